```python
import jax, jax.numpy as jnp
from jax import lax
import numpy as np

D_MODEL = 1024
BATCH = 4
SEQ = 4096
DEPTH = 2
DEC_BATCH = 128
DEC_SEQ = 4
PAST_LEN = 2048
PAGE_SIZE = 128

N_A_LAYERS = DEPTH // 2
N_B_LAYERS = DEPTH - N_A_LAYERS
N_HEADS = 16
HEAD_DIM = D_MODEL // N_HEADS
MOBA_BLOCK = 256
MOBA_TOP_K = 3
Q_BLOCK = 64
CHUNK = 128
D_SGU = D_MODEL
N_SGU_GROUPS = 8
SGU_GROUP_DIM = D_SGU // N_SGU_GROUPS
D_FF = -(-8 * D_MODEL // (3 * 256)) * 256
EPS = 1e-6

kernel_name = "yoco_sgu_moba_decoder_step"


def rms_norm(x, g):
    xf = x.astype(jnp.float32)
    y = xf * lax.rsqrt(jnp.mean(xf * xf, axis=-1, keepdims=True) + EPS)
    return (y * g.astype(jnp.float32)).astype(x.dtype)


def ada_modulation(c, w, b, n):
    m = jax.nn.silu(c) @ w + b
    return jnp.split(m[:, None, :], n, axis=-1)


def modulate(x, g, shift, scale):
    return rms_norm(x, g) * (1 + scale) + shift


def swiglu(h, w_in, w_out):
    a, b = jnp.split(h @ w_in, 2, axis=-1)
    return (jax.nn.silu(a) * b) @ w_out


def chunk_sgu_mixer(h, w_in, v_g, w_s, b_s, w_out):
    B, T, _ = h.shape
    u, v = jnp.split(jax.nn.gelu(h @ w_in), 2, axis=-1)
    v = rms_norm(v, v_g)
    tc = min(T, CHUNK)
    mask = jnp.tril(jnp.ones((tc, tc), dtype=bool))
    w = jnp.where(mask, w_s[:, :tc, :tc], 0).astype(v.dtype)
    vc = v.reshape(B, T // tc, tc, N_SGU_GROUPS, SGU_GROUP_DIM)
    mixed = jnp.einsum('gts,bcsgd->bctgd', w, vc) + b_s[:, :tc].T[None, None, :, :, None].astype(v.dtype)
    y = (u * mixed.reshape(B, T, D_SGU)) @ w_out
    return y, v


def alibi_slopes():
    return jnp.exp2(-8.0 * jnp.arange(1, N_HEADS + 1, dtype=jnp.float32) / N_HEADS)


def shared_kv(x, c, ada_w, ada_b, norm_g, w_kv, k_g):
    shift, scale = ada_modulation(c, ada_w, ada_b, 2)
    h = modulate(x, norm_g, shift, scale)
    k, v = jnp.split(h @ w_kv, 2, axis=-1)
    B, T, _ = x.shape
    k = rms_norm(k.reshape(B, T, N_HEADS, HEAD_DIM), k_g)
    v = v.reshape(B, T, N_HEADS, HEAD_DIM)
    return k, v


def to_blocks(parts):
    B = parts[0].shape[0]
    L = sum(p.shape[1] for p in parts)
    nb = -(-L // MOBA_BLOCK)
    pad = jnp.zeros((B, nb * MOBA_BLOCK - L, N_HEADS, HEAD_DIM), parts[0].dtype)
    k = jnp.concatenate(list(parts) + [pad], axis=1)
    return k.reshape(B, nb, MOBA_BLOCK, N_HEADS, HEAD_DIM)


def moba_query_block(q, q_start, kb, vb, k_mean, slopes):
    B, Tq = q.shape[:2]
    nb = kb.shape[1]
    t = q_start + jnp.arange(Tq, dtype=jnp.int32)
    own = q_start // MOBA_BLOCK
    gate = jnp.einsum('bqhd,bnhd->bqhn', q.astype(jnp.float32), k_mean)
    gate = jnp.where(jnp.arange(nb) < own, gate, -jnp.inf)
    n_sel = min(MOBA_TOP_K, nb)
    _, idx = lax.top_k(gate, n_sel)
    valid = idx < own
    bi = jnp.arange(B)[:, None, None, None]
    hi = jnp.arange(N_HEADS)[None, None, :, None]
    k_sel = kb[bi, idx, :, hi]
    v_sel = vb[bi, idx, :, hi]
    k_own = lax.dynamic_index_in_dim(kb, own, axis=1, keepdims=False)
    v_own = lax.dynamic_index_in_dim(vb, own, axis=1, keepdims=False)
    offs = jnp.arange(MOBA_BLOCK, dtype=jnp.int32)
    dist_sel = (t[None, :, None, None, None] - (idx[..., None] * MOBA_BLOCK + offs)).astype(jnp.float32)
    s_sel = jnp.einsum('bqhd,bqhrkd->bqhrk', q, k_sel, preferred_element_type=jnp.float32)
    s_sel = jnp.where(valid[..., None], s_sel - slopes[:, None, None] * dist_sel, -jnp.inf)
    s_sel = s_sel.reshape(B, Tq, N_HEADS, n_sel * MOBA_BLOCK)
    dist_own = t[:, None] - (own * MOBA_BLOCK + offs)[None, :]
    s_own = jnp.einsum('bqhd,bkhd->bqhk', q, k_own, preferred_element_type=jnp.float32)
    s_own = s_own - slopes[:, None] * dist_own[:, None, :].astype(jnp.float32)
    s_own = jnp.where((dist_own >= 0)[:, None, :], s_own, -jnp.inf)
    p = jax.nn.softmax(jnp.concatenate([s_sel, s_own], axis=-1), axis=-1)
    p_sel = p[..., :n_sel * MOBA_BLOCK].reshape(B, Tq, N_HEADS, n_sel, MOBA_BLOCK).astype(vb.dtype)
    p_own = p[..., n_sel * MOBA_BLOCK:].astype(vb.dtype)
    o = (jnp.einsum('bqhrk,bqhrkd->bqhd', p_sel, v_sel, preferred_element_type=jnp.float32)
         + jnp.einsum('bqhk,bkhd->bqhd', p_own, v_own, preferred_element_type=jnp.float32))
    return o.astype(q.dtype)


def moba_attention(q, kb, vb, k_mean, q_start):
    B, T = q.shape[:2]
    slopes = alibi_slopes()
    if T <= Q_BLOCK:
        return moba_query_block(q, jnp.asarray(q_start, jnp.int32), kb, vb, k_mean, slopes)
    nqb = T // Q_BLOCK
    qb = q.reshape(B, nqb, Q_BLOCK, N_HEADS, HEAD_DIM).transpose(1, 0, 2, 3, 4)
    starts = q_start + Q_BLOCK * jnp.arange(nqb, dtype=jnp.int32)
    out = lax.map(lambda a: moba_query_block(a[0], a[1], kb, vb, k_mean, slopes), (qb, starts))
    return out.transpose(1, 0, 2, 3, 4).reshape(B, T, N_HEADS, HEAD_DIM)


def moba_mixer(h, kb, vb, k_mean, q_start, w_q, q_g, w_o):
    B, T, _ = h.shape
    q = rms_norm((h @ w_q).reshape(B, T, N_HEADS, HEAD_DIM), q_g) * (HEAD_DIM ** -0.5)
    o = moba_attention(q, kb, vb, k_mean, q_start)
    return o.reshape(B, T, N_HEADS * HEAD_DIM) @ w_o


def run_group(x, c, k_past, v_past, q_start, ada_w, ada_b, norm1_g, norm2_g, ffn_w_in, ffn_w_out,
              sgu_w_in, sgu_v_g, sgu_w_s, sgu_b_s, sgu_w_out, kv_ada_w, kv_ada_b, kv_norm_g, w_kv, k_norm_g,
              attn_w_q, q_norm_g, attn_w_o):
    sgu_states = []
    k_new = v_new = kb = vb = k_mean = None
    for l in range(DEPTH):
        sh1, sc1, g1, sh2, sc2, g2 = ada_modulation(c, ada_w[l], ada_b[l], 6)
        h = modulate(x, norm1_g[l], sh1, sc1)
        if l < N_A_LAYERS:
            y, v_state = chunk_sgu_mixer(h, sgu_w_in[l], sgu_v_g[l], sgu_w_s[l], sgu_b_s[l], sgu_w_out[l])
            sgu_states.append(v_state)
        else:
            if l == N_A_LAYERS:
                k_new, v_new = shared_kv(x, c, kv_ada_w, kv_ada_b, kv_norm_g, w_kv, k_norm_g)
                k_parts = [k_new] if k_past is None else [k_past, k_new]
                v_parts = [v_new] if v_past is None else [v_past, v_new]
                kb = to_blocks(k_parts)
                vb = to_blocks(v_parts)
                k_mean = jnp.mean(kb.astype(jnp.float32), axis=2)
            j = l - N_A_LAYERS
            y = moba_mixer(h, kb, vb, k_mean, q_start, attn_w_q[j], q_norm_g[j], attn_w_o[j])
        x = x + g1 * y
        x = x + g2 * swiglu(modulate(x, norm2_g[l], sh2, sc2), ffn_w_in[l], ffn_w_out[l])
    return x, k_new, v_new, jnp.stack(sgu_states)


def setup_inputs(seed: int = 0) -> dict:
    key = jax.random.key(seed)
    ks = jax.random.split(key, 32)
    f32 = jnp.float32

    def nrm(k, shape, scale):
        return scale * jax.random.normal(k, shape, f32)

    n_pages = PAST_LEN // PAGE_SIZE
    n_used = DEC_BATCH * n_pages
    n_pool = n_used + n_used // 4
    page_table = jax.random.permutation(ks[6], n_pool)[:n_used].reshape(DEC_BATCH, n_pages).astype(jnp.int32)
    d = D_MODEL
    return {
        "x_prompt": nrm(ks[0], (BATCH, SEQ, d), 1.0),
        "x_sample": nrm(ks[1], (DEC_BATCH, DEC_SEQ, d), 1.0),
        "c_prompt": nrm(ks[2], (BATCH, d), 1.0),
        "c_sample": nrm(ks[3], (DEC_BATCH, d), 1.0),
        "cache_k": nrm(ks[4], (n_pool, PAGE_SIZE, N_HEADS, HEAD_DIM), 1.0),
        "cache_v": nrm(ks[5], (n_pool, PAGE_SIZE, N_HEADS, HEAD_DIM), 1.0),
        "page_table": page_table,
        "ada_w": nrm(ks[7], (DEPTH, d, 6 * d), d ** -0.5),
        "ada_b": nrm(ks[8], (DEPTH, 6 * d), 0.01),
        "norm1_g": 1.0 + nrm(ks[9], (DEPTH, d), 0.05),
        "norm2_g": 1.0 + nrm(ks[10], (DEPTH, d), 0.05),
        "ffn_w_in": nrm(ks[11], (DEPTH, d, 2 * D_FF), d ** -0.5),
        "ffn_w_out": nrm(ks[12], (DEPTH, D_FF, d), D_FF ** -0.5),
        "sgu_w_in": nrm(ks[13], (N_A_LAYERS, d, 2 * D_SGU), d ** -0.5),
        "sgu_v_g": 1.0 + nrm(ks[14], (N_A_LAYERS, D_SGU), 0.05),
        "sgu_w_s": nrm(ks[15], (N_A_LAYERS, N_SGU_GROUPS, CHUNK, CHUNK), CHUNK ** -0.5),
        "sgu_b_s": nrm(ks[16], (N_A_LAYERS, N_SGU_GROUPS, CHUNK), 0.1),
        "sgu_w_out": nrm(ks[17], (N_A_LAYERS, D_SGU, d), D_SGU ** -0.5),
        "kv_ada_w": nrm(ks[18], (d, 2 * d), d ** -0.5),
        "kv_ada_b": nrm(ks[19], (2 * d,), 0.01),
        "kv_norm_g": 1.0 + nrm(ks[20], (d,), 0.05),
        "w_kv": nrm(ks[21], (d, 2 * N_HEADS * HEAD_DIM), d ** -0.5),
        "k_norm_g": 1.0 + nrm(ks[22], (HEAD_DIM,), 0.05),
        "attn_w_q": nrm(ks[23], (N_B_LAYERS, d, N_HEADS * HEAD_DIM), d ** -0.5),
        "q_norm_g": 1.0 + nrm(ks[24], (N_B_LAYERS, HEAD_DIM), 0.05),
        "attn_w_o": nrm(ks[25], (N_B_LAYERS, N_HEADS * HEAD_DIM, d), (N_HEADS * HEAD_DIM) ** -0.5),
    }


def reference(x_prompt, x_sample, c_prompt, c_sample, cache_k, cache_v, page_table,
              ada_w, ada_b, norm1_g, norm2_g, ffn_w_in, ffn_w_out,
              sgu_w_in, sgu_v_g, sgu_w_s, sgu_b_s, sgu_w_out,
              kv_ada_w, kv_ada_b, kv_norm_g, w_kv, k_norm_g,
              attn_w_q, q_norm_g, attn_w_o):
    weights = (ada_w, ada_b, norm1_g, norm2_g, ffn_w_in, ffn_w_out,
               sgu_w_in, sgu_v_g, sgu_w_s, sgu_b_s, sgu_w_out,
               kv_ada_w, kv_ada_b, kv_norm_g, w_kv, k_norm_g,
               attn_w_q, q_norm_g, attn_w_o)
    y_prompt, k_prompt, v_prompt, sgu_prompt = run_group(x_prompt, c_prompt, None, None, 0, *weights)
    n_seq, n_pages = page_table.shape
    past_len = n_pages * PAGE_SIZE
    k_past = cache_k[page_table].reshape(n_seq, past_len, N_HEADS, HEAD_DIM)
    v_past = cache_v[page_table].reshape(n_seq, past_len, N_HEADS, HEAD_DIM)
    y_sample, k_sample, v_sample, sgu_sample = run_group(x_sample, c_sample, k_past, v_past, past_len, *weights)
    return (y_prompt, y_sample, k_prompt, v_prompt, k_sample, v_sample, sgu_prompt[:, :, -CHUNK:], sgu_sample)
```

```python
import functools

import numpy as np
import jax
import jax.numpy as jnp
from jax import lax
from jax.experimental import pallas as pl
from jax.experimental.pallas import tpu as pltpu

F32 = jnp.float32
BF16 = jnp.bfloat16
EPS = 1e-6
NEG_INF = float("-inf")

V7X_VMEM_LIMIT_BYTES = 56 * 1024 * 1024
LANES = 128

N_HEADS = 16
MOBA_BLOCK = 256
MOBA_TOP_K = 3
CHUNK = 128
N_SGU_GROUPS = 8
PAGE_SIZE = 128
TOKEN_TILE = 512


def _params(n_grid_dims):
    return pltpu.CompilerParams(
        dimension_semantics=("arbitrary",) * n_grid_dims,
        vmem_limit_bytes=V7X_VMEM_LIMIT_BYTES)


def _const_spec(shape):
    n = len(shape)
    return pl.BlockSpec(shape, lambda *_: (0,) * n, pipeline_mode=pl.Buffered(1))


def _mod_spec(mods, tm, tiles_per_batch):
    if mods.ndim == 3:
        return pl.BlockSpec((None, 1, mods.shape[-1]), lambda i: (i // tiles_per_batch, 0, 0))
    return pl.BlockSpec((tm, mods.shape[-1]), lambda i: (i, 0))


def _dot(a, b):
    return jnp.dot(a, b, preferred_element_type=F32)


def _dot_nt(a, b):
    return lax.dot_general(a, b, (((1,), (1,)), ((), ())), preferred_element_type=F32)


def _split_bf16(x):
    hi = x.astype(BF16)
    lo = (x - hi.astype(F32)).astype(BF16)
    return hi, lo


def _rms(x, g):
    return x * lax.rsqrt(jnp.mean(x * x, axis=-1, keepdims=True) + EPS) * g


def _modulate(x, g, shift, scale):
    return _rms(x, g) * (1.0 + scale) + shift


def _head_rms(x, gain, pool_ref, expand_ref):
    sq_hi, sq_lo = _split_bf16(x * x)
    pool = pool_ref[...]
    ms = _dot(sq_hi, pool) + _dot(sq_lo, pool)
    r_hi, r_lo = _split_bf16(lax.rsqrt(ms + EPS))
    expand = expand_ref[...]
    return x * (_dot(r_hi, expand) + _dot(r_lo, expand)) * gain


def _ada_kernel(c_ref, w_ref, b_ref, o_ref):
    c = c_ref[...]
    s = (c * jax.nn.sigmoid(c)).astype(BF16)
    o_ref[...] = _dot(s, w_ref[...].astype(BF16)) + b_ref[...]


def _ada(c_all, w, b, tn):
    n_layers, d, n = w.shape
    m = c_all.shape[0]
    return pl.pallas_call(
        _ada_kernel,
        grid=(n_layers, n // tn),
        in_specs=[
            pl.BlockSpec((m, d), lambda l, j: (0, 0)),
            pl.BlockSpec((None, d, tn), lambda l, j: (l, 0, j)),
            pl.BlockSpec((None, 1, tn), lambda l, j: (l, 0, j)),
        ],
        out_specs=pl.BlockSpec((None, m, tn), lambda l, j: (l, 0, j)),
        out_shape=jax.ShapeDtypeStruct((n_layers, m, n), F32),
        compiler_params=_params(2),
        name="ada",
    )(c_all, w, b.reshape(n_layers, 1, n))


def _sgu_kernel(tc, x_ref, mod_ref, g_ref, win_ref, vg_ref, wmix_ref, bmix_ref, uz_ref, v_ref):
    tm, d = x_ref.shape
    c_rows = wmix_ref.shape[1]
    gd = d // N_SGU_GROUPS
    h = _modulate(x_ref[...], g_ref[...], mod_ref[:, 0:d], mod_ref[:, d:2 * d]).astype(BF16)
    uv = jax.nn.gelu(_dot(h, win_ref[...]))
    u = uv[:, :d]
    v = _rms(uv[:, d:], vg_ref[...])
    v_ref[...] = v[tm - v_ref.shape[0]:, :]
    vb = v.astype(BF16)
    r = lax.broadcasted_iota(jnp.int32, (c_rows, c_rows), 0)
    c = lax.broadcasted_iota(jnp.int32, (c_rows, c_rows), 1)
    causal = c <= r
    if tc != c_rows:
        causal = causal & ((r // tc) == (c // tc))
    for g in range(N_SGU_GROUPS):
        wg = jnp.where(causal, wmix_ref[g], 0.0).astype(BF16)
        cols = slice(g * gd, (g + 1) * gd)
        for ch in range(tm // c_rows):
            rows = slice(ch * c_rows, (ch + 1) * c_rows)
            mixed = _dot(wg, vb[rows, cols]) + bmix_ref[:, cols]
            uz_ref[rows, cols] = (u[rows, cols] * mixed).astype(BF16)


def _sgu(x, mods, tiles_per_batch, tc, norm_g, w_in, v_g, wmix, bmix, v_rows, v_shape, v_spec):
    t, d = x.shape
    tm = TOKEN_TILE
    return pl.pallas_call(
        functools.partial(_sgu_kernel, tc),
        grid=(t // tm,),
        in_specs=[
            pl.BlockSpec((tm, d), lambda i: (i, 0)),
            _mod_spec(mods, tm, tiles_per_batch),
            _const_spec((1, d)),
            _const_spec(w_in.shape),
            _const_spec((1, d)),
            _const_spec(wmix.shape),
            _const_spec(bmix.shape),
        ],
        out_specs=[pl.BlockSpec((tm, d), lambda i: (i, 0)), v_spec],
        out_shape=[jax.ShapeDtypeStruct((t, d), BF16), jax.ShapeDtypeStruct(v_shape, F32)],
        compiler_params=_params(1),
        name="sgu",
    )(x, mods, norm_g.reshape(1, d), w_in, v_g.reshape(1, d), wmix, bmix)


def _ff_chunks(d_ff):
    step = 768
    return tuple((s, min(step, d_ff - s)) for s in range(0, d_ff, step))


def _ffn_kernel(x_ref, z_ref, mod_ref, wp_ref, g_ref, win_ref, wout_ref, o_ref):
    d = x_ref.shape[1]
    d_ff = wout_ref.shape[0]
    gate1 = mod_ref[:, 2 * d:3 * d]
    shift2 = mod_ref[:, 3 * d:4 * d]
    scale2 = mod_ref[:, 4 * d:5 * d]
    gate2 = mod_ref[:, 5 * d:6 * d]
    x1 = x_ref[...] + gate1 * _dot(z_ref[...].astype(BF16), wp_ref[...])
    h = _modulate(x1, g_ref[...], shift2, scale2).astype(BF16)
    acc = None
    for s, w in _ff_chunks(d_ff):
        a = _dot(h, win_ref[:, s:s + w])
        b = _dot(h, win_ref[:, d_ff + s:d_ff + s + w])
        act = (a * jax.nn.sigmoid(a) * b).astype(BF16)
        part = _dot(act, wout_ref[s:s + w, :])
        acc = part if acc is None else acc + part
    o_ref[...] = x1 + gate2 * acc


def _ffn(x, z, mods, tiles_per_batch, w_proj, norm_g, w_in, w_out):
    t, d = x.shape
    tm = TOKEN_TILE
    return pl.pallas_call(
        _ffn_kernel,
        grid=(t // tm,),
        in_specs=[
            pl.BlockSpec((tm, d), lambda i: (i, 0)),
            pl.BlockSpec((tm, d), lambda i: (i, 0)),
            _mod_spec(mods, tm, tiles_per_batch),
            _const_spec(w_proj.shape),
            _const_spec((1, d)),
            _const_spec(w_in.shape),
            _const_spec(w_out.shape),
        ],
        out_specs=pl.BlockSpec((tm, d), lambda i: (i, 0)),
        out_shape=jax.ShapeDtypeStruct((t, d), F32),
        compiler_params=_params(1),
        name="ffn",
    )(x, z, mods, w_proj, norm_g.reshape(1, d), w_in, w_out)


def _kvq_prompt_kernel(x_ref, kvmod_ref, mod_ref, kvg_ref, n1g_ref, wk_ref, wv_ref, wqt_ref, kg_ref, qg_ref,
                       pool_ref, expand_ref, k_ref, v_ref, kb_ref, vt_ref, qt_ref, kmean_ref):
    tm, d = x_ref.shape
    dh = d // N_HEADS
    x = x_ref[...]
    hkv = _modulate(x, kvg_ref[...], kvmod_ref[:, 0:d], kvmod_ref[:, d:2 * d]).astype(BF16)
    k = _head_rms(_dot(hkv, wk_ref[...]), kg_ref[...], pool_ref, expand_ref)
    v = _dot(hkv, wv_ref[...])
    k_ref[...] = k
    v_ref[...] = v
    kb_ref[...] = k.astype(BF16)
    hq = _modulate(x, n1g_ref[...], mod_ref[:, 0:d], mod_ref[:, d:2 * d]).astype(BF16)
    qt = _dot_nt(wqt_ref[...], hq)
    q3 = qt.reshape(N_HEADS, dh, tm)
    q3 = q3 * lax.rsqrt(jnp.mean(q3 * q3, axis=1, keepdims=True) + EPS)
    qt = (q3.reshape(d, tm) * qg_ref[...]).astype(BF16)
    for i in range(tm // MOBA_BLOCK):
        rows = slice(i * MOBA_BLOCK, (i + 1) * MOBA_BLOCK)
        kmean_ref[i] = jnp.mean(k[rows, :], axis=0, keepdims=True)
        vt_ref[i] = v[rows, :].T.astype(BF16)
        qt_ref[i] = qt[:, rows]


def _kvq_prompt(x, batch, kvmods, mods, kv_norm_g, norm1_g, w_k, w_v, w_qt, k_gain, q_gain_col, pool, expand):
    t, d = x.shape
    tm = TOKEN_TILE
    seq = t // batch
    tiles_per_batch = seq // tm
    nblk = tm // MOBA_BLOCK
    n_blocks = seq // MOBA_BLOCK
    row_spec = pl.BlockSpec((tm, d), lambda i: (i, 0))
    t_spec = pl.BlockSpec((None, nblk, d, MOBA_BLOCK),
                          lambda i: (i // tiles_per_batch, i % tiles_per_batch, 0, 0))
    return pl.pallas_call(
        _kvq_prompt_kernel,
        grid=(t // tm,),
        in_specs=[
            row_spec,
            _mod_spec(kvmods, tm, tiles_per_batch),
            _mod_spec(mods, tm, tiles_per_batch),
            _const_spec((1, d)), _const_spec((1, d)),
            _const_spec(w_k.shape), _const_spec(w_v.shape), _const_spec(w_qt.shape),
            _const_spec((1, d)), _const_spec((d, 1)),
            _const_spec(pool.shape), _const_spec(expand.shape),
        ],
        out_specs=[
            row_spec, row_spec, row_spec, t_spec, t_spec,
            pl.BlockSpec((nblk, 1, d), lambda i: (i, 0, 0)),
        ],
        out_shape=[
            jax.ShapeDtypeStruct((t, d), F32),
            jax.ShapeDtypeStruct((t, d), F32),
            jax.ShapeDtypeStruct((t, d), BF16),
            jax.ShapeDtypeStruct((batch, n_blocks, d, MOBA_BLOCK), BF16),
            jax.ShapeDtypeStruct((batch, n_blocks, d, MOBA_BLOCK), BF16),
            jax.ShapeDtypeStruct((t // MOBA_BLOCK, 1, d), F32),
        ],
        compiler_params=_params(1),
        name="kvq_prompt",
    )(x, kvmods, mods, kv_norm_g.reshape(1, d), norm1_g.reshape(1, d), w_k, w_v, w_qt,
      k_gain, q_gain_col, pool, expand)


def _kvq_sample_kernel(x_ref, kvmod_ref, mod_ref, kvg_ref, n1g_ref, wk_ref, wv_ref, wq_ref, kg_ref, qg_ref,
                       pool_ref, expand_ref, k_ref, v_ref, q_ref):
    d = x_ref.shape[1]
    x = x_ref[...]
    hkv = _modulate(x, kvg_ref[...], kvmod_ref[:, 0:d], kvmod_ref[:, d:2 * d]).astype(BF16)
    k_ref[...] = _head_rms(_dot(hkv, wk_ref[...]), kg_ref[...], pool_ref, expand_ref)
    v_ref[...] = _dot(hkv, wv_ref[...])
    hq = _modulate(x, n1g_ref[...], mod_ref[:, 0:d], mod_ref[:, d:2 * d]).astype(BF16)
    q_ref[...] = _head_rms(_dot(hq, wq_ref[...]), qg_ref[...], pool_ref, expand_ref)


def _kvq_sample(x, kvmods, mods, kv_norm_g, norm1_g, w_k, w_v, w_q, k_gain, q_gain, pool, expand):
    t, d = x.shape
    tm = TOKEN_TILE
    row_spec = pl.BlockSpec((tm, d), lambda i: (i, 0))
    return pl.pallas_call(
        _kvq_sample_kernel,
        grid=(t // tm,),
        in_specs=[
            row_spec,
            _mod_spec(kvmods, tm, 1),
            _mod_spec(mods, tm, 1),
            _const_spec((1, d)), _const_spec((1, d)),
            _const_spec(w_k.shape), _const_spec(w_v.shape), _const_spec(w_q.shape),
            _const_spec((1, d)), _const_spec((1, d)),
            _const_spec(pool.shape), _const_spec(expand.shape),
        ],
        out_specs=[row_spec, row_spec, row_spec],
        out_shape=[jax.ShapeDtypeStruct((t, d), F32)] * 3,
        compiler_params=_params(1),
        name="kvq_sample",
    )(x, kvmods, mods, kv_norm_g.reshape(1, d), norm1_g.reshape(1, d), w_k, w_v, w_q,
      k_gain, q_gain, pool, expand)


def _top_k_additive_mask(gate, own, axis):
    nb = gate.shape[axis]
    idx = lax.broadcasted_iota(jnp.int32, gate.shape, axis)
    eligible = idx < own
    g = jnp.where(eligible, gate, NEG_INF)
    mask = jnp.full(gate.shape, NEG_INF, F32)
    for _ in range(min(MOBA_TOP_K, nb)):
        best = jnp.max(g, axis=axis, keepdims=True)
        first = jnp.min(jnp.where(g == best, idx, nb), axis=axis, keepdims=True)
        pick = idx == first
        mask = jnp.where(pick, 0.0, mask)
        g = jnp.where(pick, NEG_INF, g)
    return jnp.where(eligible, mask, NEG_INF)


def _attn_prompt_kernel(slopes_ref, qt_ref, kb_ref, vt_ref, kmean_ref, o_ref, sel_ref):
    hp = pl.program_id(1)
    j = pl.program_id(2)
    blk = MOBA_BLOCK
    dh = qt_ref.shape[0] // 2
    qt = qt_ref[...]
    zeros = jnp.zeros((dh, blk), BF16)
    km_hi, km_lo = _split_bf16(kmean_ref[...])
    rel = (lax.broadcasted_iota(jnp.int32, (blk, blk), 0)
           - lax.broadcasted_iota(jnp.int32, (blk, blk), 1))
    relf = rel.astype(F32)
    outs = []
    for hh in range(2):
        slope = slopes_ref[2 * hp + hh]
        qh = jnp.concatenate([qt[:dh], zeros] if hh == 0 else [zeros, qt[dh:]], axis=0)
        gate = _dot(km_hi, qh) + _dot(km_lo, qh)
        sel_ref[...] = _top_k_additive_mask(gate, j, 0)
        srel = slope * relf
        s = jnp.where(rel <= 0, _dot(kb_ref[j], qh) + srel, NEG_INF)
        m = jnp.max(s, axis=0, keepdims=True)
        p = jnp.exp(s - m)
        l = jnp.sum(p, axis=0, keepdims=True)
        acc = _dot(vt_ref[j, hh * dh:(hh + 1) * dh, :], p.astype(BF16))

        def body(n, carry):
            m, l, acc = carry
            shift = slope * ((n - j) * blk).astype(F32)
            s = _dot(kb_ref[n], qh) + srel + (sel_ref[pl.ds(n, 1), :] + shift)
            m_new = jnp.maximum(m, jnp.max(s, axis=0, keepdims=True))
            alpha = jnp.exp(m - m_new)
            p = jnp.exp(s - m_new)
            l = alpha * l + jnp.sum(p, axis=0, keepdims=True)
            acc = alpha * acc + _dot(vt_ref[n, hh * dh:(hh + 1) * dh, :], p.astype(BF16))
            return m_new, l, acc

        m, l, acc = lax.fori_loop(0, j, body, (m, l, acc))
        outs.append(acc / l)
    o_ref[...] = jnp.concatenate(outs, axis=0).T.astype(o_ref.dtype)


def _attn_prompt(qt, kb, vt, kmean, slopes):
    batch, n_blocks, d, blk = qt.shape
    pair = 2 * (d // N_HEADS)
    return pl.pallas_call(
        _attn_prompt_kernel,
        grid=(batch, d // pair, n_blocks),
        in_specs=[
            pl.BlockSpec(memory_space=pltpu.SMEM),
            pl.BlockSpec((None, None, pair, blk), lambda b, hp, j: (b, j, hp, 0)),
            pl.BlockSpec((None, n_blocks, blk, pair), lambda b, hp, j: (b, 0, 0, hp)),
            pl.BlockSpec((None, n_blocks, pair, blk), lambda b, hp, j: (b, 0, hp, 0)),
            pl.BlockSpec((None, n_blocks, pair), lambda b, hp, j: (b, 0, hp)),
        ],
        out_specs=pl.BlockSpec((None, blk, pair), lambda b, hp, j: (b, j, hp)),
        out_shape=jax.ShapeDtypeStruct((batch, n_blocks * blk, d), BF16),
        scratch_shapes=[pltpu.VMEM((n_blocks, blk), F32)],
        compiler_params=_params(3),
        name="attn_prompt",
    )(slopes, qt, kb, vt, kmean)


def _attn_sample_kernel(n_pages, pt_ref, q_ref, kn_ref, vn_ref, slope_ref, *refs):
    k_pages = refs[:n_pages]
    v_pages = refs[n_pages:2 * n_pages]
    o_ref = refs[2 * n_pages]
    nq, d = q_ref.shape
    dh = d // N_HEADS
    rows = nq * N_HEADS
    pages_per_block = MOBA_BLOCK // PAGE_SIZE
    n_past = n_pages // pages_per_block
    past_len = n_pages * PAGE_SIZE

    q = q_ref[...]
    lane_head = lax.broadcasted_iota(jnp.int32, (N_HEADS, d), 1) // dh
    head_mask = lane_head == lax.broadcasted_iota(jnp.int32, (N_HEADS, d), 0)
    qrows = jnp.concatenate([jnp.where(head_mask, q[i:i + 1, :], 0.0) for i in range(nq)], axis=0)
    qrows_b = qrows.astype(BF16)
    row_head_mask = jnp.concatenate([head_mask] * nq, axis=0)
    slope = slope_ref[...]
    row_q = lax.broadcasted_iota(jnp.int32, (rows, 1), 0) // N_HEADS

    scores = []
    page_sums = []
    for p in range(n_pages):
        kp = k_pages[p][...]
        page_sums.append(jnp.sum(kp, axis=0, keepdims=True))
        scores.append(_dot_nt(qrows_b, kp.astype(BF16)))
    kmean = jnp.concatenate(
        [sum(page_sums[n * pages_per_block:(n + 1) * pages_per_block]) for n in range(n_past)],
        axis=0) * (1.0 / MOBA_BLOCK)
    q_hi, q_lo = _split_bf16(qrows)
    km_hi, km_lo = _split_bf16(kmean)
    gate = _dot_nt(q_hi, km_hi) + _dot_nt(q_hi, km_lo) + _dot_nt(q_lo, km_hi)
    sel = _top_k_additive_mask(gate, n_past, 1)

    dist0 = (past_len + row_q).astype(F32)
    offs = lax.broadcasted_iota(jnp.int32, (1, PAGE_SIZE), 1)
    masked = []
    for p in range(n_pages):
        n = p // pages_per_block
        kpos = (p * PAGE_SIZE + offs).astype(F32)
        masked.append(scores[p] - slope * (dist0 - kpos) + sel[:, n:n + 1])
    kn = kn_ref[...]
    own = []
    for t in range(nq):
        s = jnp.sum(qrows * kn[t:t + 1, :], axis=1, keepdims=True)
        s = s - slope * (row_q - t).astype(F32)
        own.append(jnp.where(row_q >= t, s, NEG_INF))
    m = own[0]
    for s in own[1:]:
        m = jnp.maximum(m, s)
    for s in masked:
        m = jnp.maximum(m, jnp.max(s, axis=1, keepdims=True))
    l = jnp.zeros((rows, 1), F32)
    acc = jnp.zeros((rows, d), F32)
    vn = vn_ref[...]
    for t in range(nq):
        p_own = jnp.exp(own[t] - m)
        l = l + p_own
        acc = acc + p_own * vn[t:t + 1, :]
    for p in range(n_pages):
        pr = jnp.exp(masked[p] - m)
        l = l + jnp.sum(pr, axis=1, keepdims=True)
        acc = acc + _dot(pr.astype(BF16), v_pages[p][...].astype(BF16))
    o_rows = jnp.where(row_head_mask, acc / l, 0.0)
    o_ref[...] = jnp.sum(o_rows.reshape(nq, N_HEADS, d), axis=1)


def _attn_sample(page_table, q, k_new, v_new, slope_rows, cache_k, cache_v):
    n_seq, nq, d = q.shape
    n_pages = page_table.shape[1]
    pool_pages = cache_k.shape[0]
    ck = cache_k.reshape(pool_pages, PAGE_SIZE, d)
    cv = cache_v.reshape(pool_pages, PAGE_SIZE, d)
    seq_spec = pl.BlockSpec((None, nq, d), lambda b, pt: (b, 0, 0))

    def page_spec(i):
        return pl.BlockSpec((None, PAGE_SIZE, d), lambda b, pt: (pt[b * n_pages + i], 0, 0))

    return pl.pallas_call(
        functools.partial(_attn_sample_kernel, n_pages),
        grid_spec=pltpu.PrefetchScalarGridSpec(
            num_scalar_prefetch=1,
            grid=(n_seq,),
            in_specs=[seq_spec, seq_spec, seq_spec,
                      pl.BlockSpec(slope_rows.shape, lambda b, pt: (0, 0))]
                     + [page_spec(i) for i in range(n_pages)] * 2,
            out_specs=seq_spec,
        ),
        out_shape=jax.ShapeDtypeStruct((n_seq, nq, d), F32),
        compiler_params=_params(1),
        name="attn_sample",
    )(page_table.reshape(-1), q, k_new, v_new, slope_rows, *([ck] * n_pages), *([cv] * n_pages))


def kernel(x_prompt, x_sample, c_prompt, c_sample, cache_k, cache_v, page_table, ada_w, ada_b, norm1_g, norm2_g,
           ffn_w_in, ffn_w_out, sgu_w_in, sgu_v_g, sgu_w_s, sgu_b_s, sgu_w_out, kv_ada_w, kv_ada_b, kv_norm_g,
           w_kv, k_norm_g, attn_w_q, q_norm_g, attn_w_o):
    batch, seq, d = x_prompt.shape
    n_seq, nq, _ = x_sample.shape
    dh = d // N_HEADS
    gd = d // N_SGU_GROUPS
    n_tok_p = batch * seq
    n_tok_s = n_seq * nq
    assert seq % TOKEN_TILE == 0 and n_tok_s % TOKEN_TILE == 0 and CHUNK % nq == 0

    n_c = batch + n_seq
    pad = (-n_c) % 8
    c_all = jnp.concatenate([c_prompt, c_sample, jnp.zeros((pad, d), F32)], axis=0)
    mods = _ada(c_all, ada_w, ada_b, 1536)
    kvmods = _ada(c_all, kv_ada_w[None], kv_ada_b[None], 1024)[0]
    mods_p = [mods[l, :batch][:, None, :] for l in range(2)]
    mods_s = [jnp.repeat(mods[l, batch:n_c], nq, axis=0) for l in range(2)]
    kvmods_p = kvmods[:batch][:, None, :]
    kvmods_s = jnp.repeat(kvmods[batch:n_c], nq, axis=0)

    bf = lambda w: w.astype(BF16)
    sgu_w_in_b, sgu_w_out_b = bf(sgu_w_in[0]), bf(sgu_w_out[0])
    ffn_in_b, ffn_out_b = bf(ffn_w_in), bf(ffn_w_out)
    w_k_b, w_v_b = bf(w_kv[:, :d]), bf(w_kv[:, d:])
    w_q_b, w_qt_b, w_o_b = bf(attn_w_q[0]), bf(attn_w_q[0].T), bf(attn_w_o[0])
    k_gain = jnp.tile(k_norm_g, N_HEADS).reshape(1, d)
    q_gain = (jnp.tile(q_norm_g[0], N_HEADS) * (dh ** -0.5)).reshape(1, d)
    head_of_lane = np.arange(d) // dh
    pool = jnp.asarray((head_of_lane[:, None] == np.arange(LANES)[None, :]) / dh, BF16)
    expand = jnp.asarray(np.arange(LANES)[:, None] == head_of_lane[None, :], BF16)
    slopes = jnp.exp2(-8.0 * jnp.arange(1, N_HEADS + 1, dtype=F32) / N_HEADS)

    wmix_p = sgu_w_s[0]
    bmix_p = jnp.repeat(sgu_b_s[0].T, gd, axis=1)
    wmix_s = jnp.tile(sgu_w_s[0][:, :nq, :nq], (1, CHUNK // nq, CHUNK // nq))
    bmix_s = jnp.tile(jnp.repeat(sgu_b_s[0][:, :nq].T, gd, axis=1), (CHUNK // nq, 1))

    xp = x_prompt.reshape(n_tok_p, d)
    xs = x_sample.reshape(n_tok_s, d)
    tiles_pb = seq // TOKEN_TILE

    uz_p, sguv_p = _sgu(xp, mods_p[0], tiles_pb, CHUNK, norm1_g[0], sgu_w_in_b, sgu_v_g[0], wmix_p, bmix_p,
                        CHUNK, (batch, CHUNK, d),
                        pl.BlockSpec((None, CHUNK, d), lambda i: (i // tiles_pb, 0, 0)))
    x2_p = _ffn(xp, uz_p, mods_p[0], tiles_pb, sgu_w_out_b, norm2_g[0], ffn_in_b[0], ffn_out_b[0])
    k_p, v_p, kb_p, vt_p, qt_p, kmean_p = _kvq_prompt(
        x2_p, batch, kvmods_p, mods_p[1], kv_norm_g, norm1_g[1], w_k_b, w_v_b, w_qt_b, k_gain,
        q_gain.reshape(d, 1), pool, expand)
    n_blocks = seq // MOBA_BLOCK
    o_p = _attn_prompt(qt_p, kb_p.reshape(batch, n_blocks, MOBA_BLOCK, d), vt_p,
                       kmean_p.reshape(batch, n_blocks, d), slopes)
    y_p = _ffn(x2_p, o_p.reshape(n_tok_p, d), mods_p[1], tiles_pb, w_o_b, norm2_g[1], ffn_in_b[1], ffn_out_b[1])

    uz_s, sguv_s = _sgu(xs, mods_s[0], 1, nq, norm1_g[0], sgu_w_in_b, sgu_v_g[0], wmix_s, bmix_s,
                        TOKEN_TILE, (n_tok_s, d), pl.BlockSpec((TOKEN_TILE, d), lambda i: (i, 0)))
    x2_s = _ffn(xs, uz_s, mods_s[0], 1, sgu_w_out_b, norm2_g[0], ffn_in_b[0], ffn_out_b[0])
    k_s, v_s, q_s = _kvq_sample(x2_s, kvmods_s, mods_s[1], kv_norm_g, norm1_g[1], w_k_b, w_v_b, w_q_b,
                                k_gain, q_gain, pool, expand)
    slope_rows = jnp.tile(slopes, nq).reshape(nq * N_HEADS, 1)
    o_s = _attn_sample(page_table, q_s.reshape(n_seq, nq, d), k_s.reshape(n_seq, nq, d),
                       v_s.reshape(n_seq, nq, d), slope_rows, cache_k, cache_v)
    y_s = _ffn(x2_s, o_s.reshape(n_tok_s, d), mods_s[1], 1, w_o_b, norm2_g[1], ffn_in_b[1], ffn_out_b[1])

    return (y_p.reshape(batch, seq, d), y_s.reshape(n_seq, nq, d),
            k_p.reshape(batch, seq, N_HEADS, dh), v_p.reshape(batch, seq, N_HEADS, dh),
            k_s.reshape(n_seq, nq, N_HEADS, dh), v_s.reshape(n_seq, nq, N_HEADS, dh),
            sguv_p.reshape(1, batch, CHUNK, d), sguv_s.reshape(1, n_seq, nq, d))
```

```python
import functools

import numpy as np
import jax
import jax.numpy as jnp
from jax import lax
from jax.experimental import pallas as pl
from jax.experimental.pallas import tpu as pltpu

F32 = jnp.float32
BF16 = jnp.bfloat16
EPS = 1e-6
NEG_INF = float("-inf")
LOG2_E = 1.4426950408889634

V7X_VMEM_LIMIT_BYTES = 56 * 1024 * 1024
LANES = 128

N_HEADS = 16
MOBA_BLOCK = 256
MOBA_TOP_K = 3
CHUNK = 128
N_SGU_GROUPS = 8
PAGE_SIZE = 128
TOKEN_TILE = 512


def _params(n_grid_dims):
    return pltpu.CompilerParams(
        dimension_semantics=("arbitrary",) * n_grid_dims,
        vmem_limit_bytes=V7X_VMEM_LIMIT_BYTES)


def _const_spec(shape):
    n = len(shape)
    return pl.BlockSpec(shape, lambda *_: (0,) * n, pipeline_mode=pl.Buffered(1))


def _mod_spec(mods, tm, tiles_per_batch):
    if mods.ndim == 3:
        return pl.BlockSpec((None, 1, mods.shape[-1]), lambda i: (i // tiles_per_batch, 0, 0))
    return pl.BlockSpec((tm, mods.shape[-1]), lambda i: (i, 0))


def _dot(a, b):
    return jnp.dot(a, b, preferred_element_type=F32)


def _dot_nt(a, b):
    return lax.dot_general(a, b, (((1,), (1,)), ((), ())), preferred_element_type=F32)


def _split_bf16(x):
    hi = x.astype(BF16)
    lo = (x - hi.astype(F32)).astype(BF16)
    return hi, lo


def _rms(x, g):
    return x * lax.rsqrt(jnp.mean(x * x, axis=-1, keepdims=True) + EPS) * g


def _modulate(x, g, shift, scale):
    return _rms(x, g) * (1.0 + scale) + shift


def _head_rms(x, gain, pool_ref, expand_ref):
    sq_hi, sq_lo = _split_bf16(x * x)
    pool = pool_ref[...]
    ms = _dot(sq_hi, pool) + _dot(sq_lo, pool)
    r_hi, r_lo = _split_bf16(lax.rsqrt(ms + EPS))
    expand = expand_ref[...]
    return x * (_dot(r_hi, expand) + _dot(r_lo, expand)) * gain


def _ada_kernel(c_ref, w_ref, b_ref, o_ref):
    c = c_ref[...]
    s = (c * jax.nn.sigmoid(c)).astype(BF16)
    o_ref[...] = _dot(s, w_ref[...].astype(BF16)) + b_ref[...]


def _ada(c_all, w, b, tn):
    n_layers, d, n = w.shape
    m = c_all.shape[0]
    return pl.pallas_call(
        _ada_kernel,
        grid=(n_layers, n // tn),
        in_specs=[
            pl.BlockSpec((m, d), lambda l, j: (0, 0)),
            pl.BlockSpec((None, d, tn), lambda l, j: (l, 0, j)),
            pl.BlockSpec((None, 1, tn), lambda l, j: (l, 0, j)),
        ],
        out_specs=pl.BlockSpec((None, m, tn), lambda l, j: (l, 0, j)),
        out_shape=jax.ShapeDtypeStruct((n_layers, m, n), F32),
        compiler_params=_params(2),
        name="ada",
    )(c_all, w, b.reshape(n_layers, 1, n))


def _sgu_kernel(tc, x_ref, mod_ref, g_ref, win_ref, vg_ref, wmix_ref, bmix_ref, uz_ref, v_ref):
    tm, d = x_ref.shape
    c_rows = wmix_ref.shape[1]
    gd = d // N_SGU_GROUPS
    h = _modulate(x_ref[...], g_ref[...], mod_ref[:, 0:d], mod_ref[:, d:2 * d]).astype(BF16)
    uv = jax.nn.gelu(_dot(h, win_ref[...]))
    u = uv[:, :d]
    v = _rms(uv[:, d:], vg_ref[...])
    v_ref[...] = v[tm - v_ref.shape[0]:, :]
    vb = v.astype(BF16)
    r = lax.broadcasted_iota(jnp.int32, (c_rows, c_rows), 0)
    c = lax.broadcasted_iota(jnp.int32, (c_rows, c_rows), 1)
    causal = c <= r
    if tc != c_rows:
        causal = causal & ((r // tc) == (c // tc))
    for g in range(N_SGU_GROUPS):
        wg = jnp.where(causal, wmix_ref[g], 0.0).astype(BF16)
        cols = slice(g * gd, (g + 1) * gd)
        for ch in range(tm // c_rows):
            rows = slice(ch * c_rows, (ch + 1) * c_rows)
            mixed = _dot(wg, vb[rows, cols]) + bmix_ref[:, cols]
            uz_ref[rows, cols] = (u[rows, cols] * mixed).astype(BF16)


def _sgu(x, mods, tiles_per_batch, tc, norm_g, w_in, v_g, wmix, bmix, v_rows, v_shape, v_spec):
    t, d = x.shape
    tm = TOKEN_TILE
    return pl.pallas_call(
        functools.partial(_sgu_kernel, tc),
        grid=(t // tm,),
        in_specs=[
            pl.BlockSpec((tm, d), lambda i: (i, 0)),
            _mod_spec(mods, tm, tiles_per_batch),
            _const_spec((1, d)),
            _const_spec(w_in.shape),
            _const_spec((1, d)),
            _const_spec(wmix.shape),
            _const_spec(bmix.shape),
        ],
        out_specs=[pl.BlockSpec((tm, d), lambda i: (i, 0)), v_spec],
        out_shape=[jax.ShapeDtypeStruct((t, d), BF16), jax.ShapeDtypeStruct(v_shape, F32)],
        compiler_params=_params(1),
        name="sgu",
    )(x, mods, norm_g.reshape(1, d), w_in, v_g.reshape(1, d), wmix, bmix)


def _ff_chunks(d_ff):
    step = 768
    return tuple((s, min(step, d_ff - s)) for s in range(0, d_ff, step))


def _ffn_kernel(x_ref, z_ref, mod_ref, wp_ref, g_ref, win_ref, wout_ref, o_ref):
    d = x_ref.shape[1]
    d_ff = wout_ref.shape[0]
    gate1 = mod_ref[:, 2 * d:3 * d]
    shift2 = mod_ref[:, 3 * d:4 * d]
    scale2 = mod_ref[:, 4 * d:5 * d]
    gate2 = mod_ref[:, 5 * d:6 * d]
    x1 = x_ref[...] + gate1 * _dot(z_ref[...].astype(BF16), wp_ref[...])
    h = _modulate(x1, g_ref[...], shift2, scale2).astype(BF16)
    acc = None
    for s, w in _ff_chunks(d_ff):
        a = _dot(h, win_ref[:, s:s + w])
        b = _dot(h, win_ref[:, d_ff + s:d_ff + s + w])
        act = (a * jax.nn.sigmoid(a) * b).astype(BF16)
        part = _dot(act, wout_ref[s:s + w, :])
        acc = part if acc is None else acc + part
    o_ref[...] = x1 + gate2 * acc


def _ffn(x, z, mods, tiles_per_batch, w_proj, norm_g, w_in, w_out):
    t, d = x.shape
    tm = TOKEN_TILE
    return pl.pallas_call(
        _ffn_kernel,
        grid=(t // tm,),
        in_specs=[
            pl.BlockSpec((tm, d), lambda i: (i, 0)),
            pl.BlockSpec((tm, d), lambda i: (i, 0)),
            _mod_spec(mods, tm, tiles_per_batch),
            _const_spec(w_proj.shape),
            _const_spec((1, d)),
            _const_spec(w_in.shape),
            _const_spec(w_out.shape),
        ],
        out_specs=pl.BlockSpec((tm, d), lambda i: (i, 0)),
        out_shape=jax.ShapeDtypeStruct((t, d), F32),
        compiler_params=_params(1),
        name="ffn",
    )(x, z, mods, w_proj, norm_g.reshape(1, d), w_in, w_out)


def _head_rms_t(xt, gain_col):
    d, tm = xt.shape
    x3 = xt.reshape(N_HEADS, d // N_HEADS, tm)
    x3 = x3 * lax.rsqrt(jnp.mean(x3 * x3, axis=1, keepdims=True) + EPS)
    return x3.reshape(d, tm) * gain_col


def _kvq_prompt_kernel(x_ref, kvmod_ref, mod_ref, kvg_ref, n1g_ref, wkt_ref, wvt_ref, wqt_ref, kg_ref, qg_ref,
                       kt_ref, vt_ref, kb_ref, vtb_ref, qt_ref, kmean_ref):
    tm, d = x_ref.shape
    x = x_ref[...]
    hkv = _modulate(x, kvg_ref[...], kvmod_ref[:, 0:d], kvmod_ref[:, d:2 * d]).astype(BF16)
    kt = _head_rms_t(_dot_nt(wkt_ref[...], hkv), kg_ref[...])
    vt = _dot_nt(wvt_ref[...], hkv)
    kt_ref[...] = kt
    vt_ref[...] = vt
    vtb_ref[...] = vt.astype(BF16)
    hq = _modulate(x, n1g_ref[...], mod_ref[:, 0:d], mod_ref[:, d:2 * d]).astype(BF16)
    qt_ref[...] = _head_rms_t(_dot_nt(wqt_ref[...], hq), qg_ref[...]).astype(BF16)
    for i in range(tm // MOBA_BLOCK):
        cols = slice(i * MOBA_BLOCK, (i + 1) * MOBA_BLOCK)
        k_nat = kt[:, cols].T
        kmean_ref[i] = jnp.mean(k_nat, axis=0, keepdims=True)
        kb_ref[cols, :] = k_nat.astype(BF16)


def _kvq_prompt(x, batch, kvmods, mods, kv_norm_g, norm1_g, w_kt, w_vt, w_qt, k_gain_col, q_gain_col):
    t, d = x.shape
    tm = TOKEN_TILE
    seq = t // batch
    tiles_per_batch = seq // tm
    nblk = tm // MOBA_BLOCK
    n_blocks = seq // MOBA_BLOCK
    row_spec = pl.BlockSpec((tm, d), lambda i: (i, 0))
    col_spec = pl.BlockSpec((None, d, tm), lambda i: (i // tiles_per_batch, 0, i % tiles_per_batch))
    return pl.pallas_call(
        _kvq_prompt_kernel,
        grid=(t // tm,),
        in_specs=[
            row_spec,
            _mod_spec(kvmods, tm, tiles_per_batch),
            _mod_spec(mods, tm, tiles_per_batch),
            _const_spec((1, d)), _const_spec((1, d)),
            _const_spec(w_kt.shape), _const_spec(w_vt.shape), _const_spec(w_qt.shape),
            _const_spec((d, 1)), _const_spec((d, 1)),
        ],
        out_specs=[
            col_spec, col_spec, row_spec, col_spec, col_spec,
            pl.BlockSpec((nblk, 1, d), lambda i: (i, 0, 0)),
        ],
        out_shape=[
            jax.ShapeDtypeStruct((batch, d, seq), F32),
            jax.ShapeDtypeStruct((batch, d, seq), F32),
            jax.ShapeDtypeStruct((t, d), BF16),
            jax.ShapeDtypeStruct((batch, d, seq), BF16),
            jax.ShapeDtypeStruct((batch, d, seq), BF16),
            jax.ShapeDtypeStruct((t // MOBA_BLOCK, 1, d), F32),
        ],
        compiler_params=_params(1),
        name="kvq_prompt",
    )(x, kvmods, mods, kv_norm_g.reshape(1, d), norm1_g.reshape(1, d), w_kt, w_vt, w_qt,
      k_gain_col, q_gain_col)


def _kvq_sample_kernel(x_ref, kvmod_ref, mod_ref, kvg_ref, n1g_ref, wk_ref, wv_ref, wq_ref, kg_ref, qg_ref,
                       pool_ref, expand_ref, k_ref, v_ref, q_ref):
    d = x_ref.shape[1]
    x = x_ref[...]
    hkv = _modulate(x, kvg_ref[...], kvmod_ref[:, 0:d], kvmod_ref[:, d:2 * d]).astype(BF16)
    k_ref[...] = _head_rms(_dot(hkv, wk_ref[...]), kg_ref[...], pool_ref, expand_ref)
    v_ref[...] = _dot(hkv, wv_ref[...])
    hq = _modulate(x, n1g_ref[...], mod_ref[:, 0:d], mod_ref[:, d:2 * d]).astype(BF16)
    q_ref[...] = _head_rms(_dot(hq, wq_ref[...]), qg_ref[...], pool_ref, expand_ref)


def _kvq_sample(x, kvmods, mods, kv_norm_g, norm1_g, w_k, w_v, w_q, k_gain, q_gain, pool, expand):
    t, d = x.shape
    tm = TOKEN_TILE
    row_spec = pl.BlockSpec((tm, d), lambda i: (i, 0))
    return pl.pallas_call(
        _kvq_sample_kernel,
        grid=(t // tm,),
        in_specs=[
            row_spec,
            _mod_spec(kvmods, tm, 1),
            _mod_spec(mods, tm, 1),
            _const_spec((1, d)), _const_spec((1, d)),
            _const_spec(w_k.shape), _const_spec(w_v.shape), _const_spec(w_q.shape),
            _const_spec((1, d)), _const_spec((1, d)),
            _const_spec(pool.shape), _const_spec(expand.shape),
        ],
        out_specs=[row_spec, row_spec, row_spec],
        out_shape=[jax.ShapeDtypeStruct((t, d), F32)] * 3,
        compiler_params=_params(1),
        name="kvq_sample",
    )(x, kvmods, mods, kv_norm_g.reshape(1, d), norm1_g.reshape(1, d), w_k, w_v, w_q,
      k_gain, q_gain, pool, expand)


def _top_k_additive_mask(gate, own, axis):
    nb = gate.shape[axis]
    idx = lax.broadcasted_iota(jnp.int32, gate.shape, axis)
    eligible = idx < own
    g = jnp.where(eligible, gate, NEG_INF)
    mask = jnp.full(gate.shape, NEG_INF, F32)
    for _ in range(min(MOBA_TOP_K, nb)):
        best = jnp.max(g, axis=axis, keepdims=True)
        first = jnp.min(jnp.where(g == best, idx, nb), axis=axis, keepdims=True)
        pick = idx == first
        mask = jnp.where(pick, 0.0, mask)
        g = jnp.where(pick, NEG_INF, g)
    return jnp.where(eligible, mask, NEG_INF)


def _attn_prompt_kernel(j, slopes_ref, qt_ref, kb_ref, vt_ref, kmean_ref, o_ref):
    hp = pl.program_id(1)
    blk = MOBA_BLOCK
    dh = qt_ref.shape[0] // 2
    qt = qt_ref[...]
    zeros = jnp.zeros((dh, blk), BF16)
    rel = (lax.broadcasted_iota(jnp.int32, (blk, blk), 0)
           - lax.broadcasted_iota(jnp.int32, (blk, blk), 1))
    relf = rel.astype(F32)
    if j > 0:
        km_hi, km_lo = _split_bf16(kmean_ref[...])
    outs = []
    for hh in range(2):
        slope = slopes_ref[2 * hp + hh]
        qh = jnp.concatenate([qt[:dh], zeros] if hh == 0 else [zeros, qt[dh:]], axis=0)
        srel = slope * relf
        rows = []
        if j > 0:
            sel = _top_k_additive_mask(_dot(km_hi, qh) + _dot(km_lo, qh), j, 0)
            rows = [sel[i:i + 1, :] + slope * float((i - j) * blk) for i in range(j)]
        rows.append(jnp.zeros((1, blk), F32))
        ts = []
        m = None
        for i in range(j + 1):
            t = _dot(kb_ref[i * blk:(i + 1) * blk, :], qh) + srel
            if i == j:
                t = jnp.where(rel <= 0, t, NEG_INF)
            ts.append(t)
            mi = jnp.max(t, axis=0, keepdims=True) + rows[i]
            m = mi if m is None else jnp.maximum(m, mi)
        l = None
        acc = None
        for i in range(j + 1):
            p = jnp.exp2(ts[i] - (m - rows[i]))
            li = jnp.sum(p, axis=0, keepdims=True)
            part = _dot(vt_ref[hh * dh:(hh + 1) * dh, i * blk:(i + 1) * blk], p.astype(BF16))
            l = li if l is None else l + li
            acc = part if acc is None else acc + part
        outs.append(acc / l)
    o_ref[...] = jnp.concatenate(outs, axis=0).T.astype(o_ref.dtype)


def _attn_prompt(qt, kb, vt, kmean, slopes):
    batch, d, seq = qt.shape
    blk = MOBA_BLOCK
    n_blocks = seq // blk
    pair = 2 * (d // N_HEADS)
    outs = []
    for j in range(n_blocks):
        keys = (j + 1) * blk
        outs.append(pl.pallas_call(
            functools.partial(_attn_prompt_kernel, j),
            grid=(batch, d // pair),
            in_specs=[
                pl.BlockSpec(memory_space=pltpu.SMEM),
                pl.BlockSpec((None, pair, blk), lambda b, hp, j=j: (b, hp, j)),
                pl.BlockSpec((None, keys, pair), lambda b, hp: (b, 0, hp)),
                pl.BlockSpec((None, pair, keys), lambda b, hp: (b, hp, 0)),
                pl.BlockSpec((None, n_blocks, pair), lambda b, hp: (b, 0, hp)),
            ],
            out_specs=pl.BlockSpec((None, blk, pair), lambda b, hp: (b, 0, hp)),
            out_shape=jax.ShapeDtypeStruct((batch, blk, d), BF16),
            compiler_params=_params(2),
            name=f"attn_prompt_{j}",
        )(slopes, qt, kb, vt, kmean))
    return jnp.concatenate(outs, axis=1)


def _attn_sample_kernel(n_pages, pt_ref, q_ref, kn_ref, vn_ref, slope_ref, *refs):
    k_pages = refs[:n_pages]
    v_pages = refs[n_pages:2 * n_pages]
    o_ref = refs[2 * n_pages]
    nq, d = q_ref.shape
    dh = d // N_HEADS
    rows = nq * N_HEADS
    pages_per_block = MOBA_BLOCK // PAGE_SIZE
    n_past = n_pages // pages_per_block
    past_len = n_pages * PAGE_SIZE

    q = q_ref[...]
    lane_head = lax.broadcasted_iota(jnp.int32, (N_HEADS, d), 1) // dh
    head_mask = lane_head == lax.broadcasted_iota(jnp.int32, (N_HEADS, d), 0)
    qrows = jnp.concatenate([jnp.where(head_mask, q[i:i + 1, :], 0.0) for i in range(nq)], axis=0)
    qrows_b = qrows.astype(BF16)
    row_head_mask = jnp.concatenate([head_mask] * nq, axis=0)
    slope = slope_ref[...]
    row_q = lax.broadcasted_iota(jnp.int32, (rows, 1), 0) // N_HEADS

    scores = [_dot(qrows_b, k_pages[p][...].astype(BF16)) for p in range(n_pages)]
    gate = jnp.concatenate(
        [sum(jnp.sum(s, axis=1, keepdims=True) for s in scores[n * pages_per_block:(n + 1) * pages_per_block])
         for n in range(n_past)], axis=1) * (1.0 / MOBA_BLOCK)
    sel = _top_k_additive_mask(gate, n_past, 1)

    dist0 = (past_len + row_q).astype(F32)
    offs = lax.broadcasted_iota(jnp.int32, (1, PAGE_SIZE), 1)
    masked = []
    for p in range(n_pages):
        n = p // pages_per_block
        kpos = (p * PAGE_SIZE + offs).astype(F32)
        masked.append(scores[p] - slope * (dist0 - kpos) + sel[:, n:n + 1])
    kn = kn_ref[...]
    own = []
    for t in range(nq):
        s = jnp.sum(qrows * kn[t:t + 1, :], axis=1, keepdims=True)
        s = s - slope * (row_q - t).astype(F32)
        own.append(jnp.where(row_q >= t, s, NEG_INF))
    m = own[0]
    for s in own[1:]:
        m = jnp.maximum(m, s)
    for s in masked:
        m = jnp.maximum(m, jnp.max(s, axis=1, keepdims=True))
    l = jnp.zeros((rows, 1), F32)
    acc = jnp.zeros((rows, d), F32)
    vn = vn_ref[...]
    for t in range(nq):
        p_own = jnp.exp(own[t] - m)
        l = l + p_own
        acc = acc + p_own * vn[t:t + 1, :]
    for p in range(n_pages):
        pr = jnp.exp(masked[p] - m)
        l = l + jnp.sum(pr, axis=1, keepdims=True)
        acc = acc + _dot_nt(pr.astype(BF16), v_pages[p][...].astype(BF16))
    o_rows = jnp.where(row_head_mask, acc / l, 0.0)
    o_ref[...] = jnp.sum(o_rows.reshape(nq, N_HEADS, d), axis=1)


def _attn_sample(page_table, q, k_new, v_new, slope_rows, cache_k, cache_v):
    n_seq, nq, d = q.shape
    n_pages = page_table.shape[1]
    pool_pages = cache_k.shape[0]
    ck = cache_k.transpose(0, 2, 3, 1).reshape(pool_pages, d, PAGE_SIZE)
    cv = cache_v.transpose(0, 2, 3, 1).reshape(pool_pages, d, PAGE_SIZE)
    seq_spec = pl.BlockSpec((None, nq, d), lambda b, pt: (b, 0, 0))

    def page_spec(i):
        return pl.BlockSpec((None, d, PAGE_SIZE), lambda b, pt: (pt[b * n_pages + i], 0, 0))

    return pl.pallas_call(
        functools.partial(_attn_sample_kernel, n_pages),
        grid_spec=pltpu.PrefetchScalarGridSpec(
            num_scalar_prefetch=1,
            grid=(n_seq,),
            in_specs=[seq_spec, seq_spec, seq_spec,
                      pl.BlockSpec(slope_rows.shape, lambda b, pt: (0, 0))]
                     + [page_spec(i) for i in range(n_pages)] * 2,
            out_specs=seq_spec,
        ),
        out_shape=jax.ShapeDtypeStruct((n_seq, nq, d), F32),
        compiler_params=_params(1),
        name="attn_sample",
    )(page_table.reshape(-1), q, k_new, v_new, slope_rows, *([ck] * n_pages), *([cv] * n_pages))


def kernel(x_prompt, x_sample, c_prompt, c_sample, cache_k, cache_v, page_table, ada_w, ada_b, norm1_g, norm2_g,
           ffn_w_in, ffn_w_out, sgu_w_in, sgu_v_g, sgu_w_s, sgu_b_s, sgu_w_out, kv_ada_w, kv_ada_b, kv_norm_g,
           w_kv, k_norm_g, attn_w_q, q_norm_g, attn_w_o):
    batch, seq, d = x_prompt.shape
    n_seq, nq, _ = x_sample.shape
    dh = d // N_HEADS
    gd = d // N_SGU_GROUPS
    n_tok_p = batch * seq
    n_tok_s = n_seq * nq
    assert seq % TOKEN_TILE == 0 and n_tok_s % TOKEN_TILE == 0 and CHUNK % nq == 0

    n_c = batch + n_seq
    pad = (-n_c) % 8
    c_all = jnp.concatenate([c_prompt, c_sample, jnp.zeros((pad, d), F32)], axis=0)
    mods = _ada(c_all, ada_w, ada_b, 1536)
    kvmods = _ada(c_all, kv_ada_w[None], kv_ada_b[None], 1024)[0]
    mods_p = [mods[l, :batch][:, None, :] for l in range(2)]
    mods_s = [jnp.repeat(mods[l, batch:n_c], nq, axis=0) for l in range(2)]
    kvmods_p = kvmods[:batch][:, None, :]
    kvmods_s = jnp.repeat(kvmods[batch:n_c], nq, axis=0)

    bf = lambda w: w.astype(BF16)
    sgu_w_in_b, sgu_w_out_b = bf(sgu_w_in[0]), bf(sgu_w_out[0])
    ffn_in_b, ffn_out_b = bf(ffn_w_in), bf(ffn_w_out)
    w_k_b, w_v_b = bf(w_kv[:, :d]), bf(w_kv[:, d:])
    w_q_b, w_qt_b, w_o_b = bf(attn_w_q[0]), bf(attn_w_q[0].T), bf(attn_w_o[0])
    k_gain = jnp.tile(k_norm_g, N_HEADS).reshape(1, d)
    q_gain = (jnp.tile(q_norm_g[0], N_HEADS) * (dh ** -0.5)).reshape(1, d)
    head_of_lane = np.arange(d) // dh
    pool = jnp.asarray((head_of_lane[:, None] == np.arange(LANES)[None, :]) / dh, BF16)
    expand = jnp.asarray(np.arange(LANES)[:, None] == head_of_lane[None, :], BF16)
    slopes = jnp.exp2(-8.0 * jnp.arange(1, N_HEADS + 1, dtype=F32) / N_HEADS)

    wmix_p = sgu_w_s[0]
    bmix_p = jnp.repeat(sgu_b_s[0].T, gd, axis=1)
    wmix_s = jnp.tile(sgu_w_s[0][:, :nq, :nq], (1, CHUNK // nq, CHUNK // nq))
    bmix_s = jnp.tile(jnp.repeat(sgu_b_s[0][:, :nq].T, gd, axis=1), (CHUNK // nq, 1))

    xp = x_prompt.reshape(n_tok_p, d)
    xs = x_sample.reshape(n_tok_s, d)
    tiles_pb = seq // TOKEN_TILE

    uz_p, sguv_p = _sgu(xp, mods_p[0], tiles_pb, CHUNK, norm1_g[0], sgu_w_in_b, sgu_v_g[0], wmix_p, bmix_p,
                        CHUNK, (batch, CHUNK, d),
                        pl.BlockSpec((None, CHUNK, d), lambda i: (i // tiles_pb, 0, 0)))
    x2_p = _ffn(xp, uz_p, mods_p[0], tiles_pb, sgu_w_out_b, norm2_g[0], ffn_in_b[0], ffn_out_b[0])
    kt_p, vt_p, kb_p, vtb_p, qt_p, kmean_p = _kvq_prompt(
        x2_p, batch, kvmods_p, mods_p[1], kv_norm_g, norm1_g[1], bf(w_kv[:, :d].T), bf(w_kv[:, d:].T), w_qt_b,
        k_gain.reshape(d, 1), (q_gain * LOG2_E).reshape(d, 1))
    o_p = _attn_prompt(qt_p, kb_p.reshape(batch, seq, d), vtb_p,
                       kmean_p.reshape(batch, seq // MOBA_BLOCK, d), slopes * LOG2_E)
    y_p = _ffn(x2_p, o_p.reshape(n_tok_p, d), mods_p[1], tiles_pb, w_o_b, norm2_g[1], ffn_in_b[1], ffn_out_b[1])

    uz_s, sguv_s = _sgu(xs, mods_s[0], 1, nq, norm1_g[0], sgu_w_in_b, sgu_v_g[0], wmix_s, bmix_s,
                        TOKEN_TILE, (n_tok_s, d), pl.BlockSpec((TOKEN_TILE, d), lambda i: (i, 0)))
    x2_s = _ffn(xs, uz_s, mods_s[0], 1, sgu_w_out_b, norm2_g[0], ffn_in_b[0], ffn_out_b[0])
    k_s, v_s, q_s = _kvq_sample(x2_s, kvmods_s, mods_s[1], kv_norm_g, norm1_g[1], w_k_b, w_v_b, w_q_b,
                                k_gain, q_gain, pool, expand)
    slope_rows = jnp.tile(slopes, nq).reshape(nq * N_HEADS, 1)
    o_s = _attn_sample(page_table, q_s.reshape(n_seq, nq, d), k_s.reshape(n_seq, nq, d),
                       v_s.reshape(n_seq, nq, d), slope_rows, cache_k, cache_v)
    y_s = _ffn(x2_s, o_s.reshape(n_tok_s, d), mods_s[1], 1, w_o_b, norm2_g[1], ffn_in_b[1], ffn_out_b[1])

    return (y_p.reshape(batch, seq, d), y_s.reshape(n_seq, nq, d),
            kt_p.reshape(batch, N_HEADS, dh, seq).transpose(0, 3, 1, 2),
            vt_p.reshape(batch, N_HEADS, dh, seq).transpose(0, 3, 1, 2),
            k_s.reshape(n_seq, nq, N_HEADS, dh), v_s.reshape(n_seq, nq, N_HEADS, dh),
            sguv_p.reshape(1, batch, CHUNK, d), sguv_s.reshape(1, n_seq, nq, d))
```

```python
import functools

import numpy as np
import jax
import jax.numpy as jnp
from jax import lax
from jax.experimental import pallas as pl
from jax.experimental.pallas import tpu as pltpu

F32 = jnp.float32
BF16 = jnp.bfloat16
EPS = 1e-6
NEG_INF = float("-inf")
LOG2_E = 1.4426950408889634

V7X_VMEM_LIMIT_BYTES = 56 * 1024 * 1024
LANES = 128

N_HEADS = 16
MOBA_BLOCK = 256
MOBA_TOP_K = 3
CHUNK = 128
N_SGU_GROUPS = 8
PAGE_SIZE = 128
TOKEN_TILE = 512


def _params(n_grid_dims):
    return pltpu.CompilerParams(
        dimension_semantics=("arbitrary",) * n_grid_dims,
        vmem_limit_bytes=V7X_VMEM_LIMIT_BYTES)


def _const_spec(shape):
    n = len(shape)
    return pl.BlockSpec(shape, lambda *_: (0,) * n, pipeline_mode=pl.Buffered(1))


def _mod_spec(mods, tm, tiles_per_batch):
    if mods.ndim == 3:
        return pl.BlockSpec((None, 1, mods.shape[-1]), lambda i: (i // tiles_per_batch, 0, 0))
    return pl.BlockSpec((tm, mods.shape[-1]), lambda i: (i, 0))


def _dot(a, b):
    return jnp.dot(a, b, preferred_element_type=F32)


def _dot_nt(a, b):
    return lax.dot_general(a, b, (((1,), (1,)), ((), ())), preferred_element_type=F32)


def _split_bf16(x):
    hi = x.astype(BF16)
    lo = (x - hi.astype(F32)).astype(BF16)
    return hi, lo


def _rms(x, g):
    return x * lax.rsqrt(jnp.mean(x * x, axis=-1, keepdims=True) + EPS) * g


def _modulate(x, g, shift, scale):
    return _rms(x, g) * (1.0 + scale) + shift


def _head_rms(x, gain, pool_ref, expand_ref):
    sq_hi, sq_lo = _split_bf16(x * x)
    pool = pool_ref[...]
    ms = _dot(sq_hi, pool) + _dot(sq_lo, pool)
    r_hi, r_lo = _split_bf16(lax.rsqrt(ms + EPS))
    expand = expand_ref[...]
    return x * (_dot(r_hi, expand) + _dot(r_lo, expand)) * gain


def _ada_kernel(c_ref, w_ref, b_ref, o_ref):
    c = c_ref[...]
    s = (c * jax.nn.sigmoid(c)).astype(BF16)
    o_ref[...] = _dot(s, w_ref[...].astype(BF16)) + b_ref[...]


def _ada(c_all, w, b, tn):
    n_layers, d, n = w.shape
    m = c_all.shape[0]
    return pl.pallas_call(
        _ada_kernel,
        grid=(n_layers, n // tn),
        in_specs=[
            pl.BlockSpec((m, d), lambda l, j: (0, 0)),
            pl.BlockSpec((None, d, tn), lambda l, j: (l, 0, j)),
            pl.BlockSpec((None, 1, tn), lambda l, j: (l, 0, j)),
        ],
        out_specs=pl.BlockSpec((None, m, tn), lambda l, j: (l, 0, j)),
        out_shape=jax.ShapeDtypeStruct((n_layers, m, n), F32),
        compiler_params=_params(2),
        name="ada",
    )(c_all, w, b.reshape(n_layers, 1, n))


def _sgu_kernel(tc, x_ref, mod_ref, g_ref, win_ref, vg_ref, wmix_ref, bmix_ref, uz_ref, v_ref):
    tm, d = x_ref.shape
    c_rows = wmix_ref.shape[1]
    gd = d // N_SGU_GROUPS
    h = _modulate(x_ref[...], g_ref[...], mod_ref[:, 0:d], mod_ref[:, d:2 * d]).astype(BF16)
    uv = jax.nn.gelu(_dot(h, win_ref[...]))
    u = uv[:, :d]
    v = _rms(uv[:, d:], vg_ref[...])
    v_ref[...] = v[tm - v_ref.shape[0]:, :]
    vb = v.astype(BF16)
    r = lax.broadcasted_iota(jnp.int32, (c_rows, c_rows), 0)
    c = lax.broadcasted_iota(jnp.int32, (c_rows, c_rows), 1)
    causal = c <= r
    if tc != c_rows:
        causal = causal & ((r // tc) == (c // tc))
    for g in range(N_SGU_GROUPS):
        wg = jnp.where(causal, wmix_ref[g], 0.0).astype(BF16)
        cols = slice(g * gd, (g + 1) * gd)
        for ch in range(tm // c_rows):
            rows = slice(ch * c_rows, (ch + 1) * c_rows)
            mixed = _dot(wg, vb[rows, cols]) + bmix_ref[:, cols]
            uz_ref[rows, cols] = (u[rows, cols] * mixed).astype(BF16)


def _sgu(x, mods, tiles_per_batch, tc, norm_g, w_in, v_g, wmix, bmix, v_rows, v_shape, v_spec):
    t, d = x.shape
    tm = TOKEN_TILE
    return pl.pallas_call(
        functools.partial(_sgu_kernel, tc),
        grid=(t // tm,),
        in_specs=[
            pl.BlockSpec((tm, d), lambda i: (i, 0)),
            _mod_spec(mods, tm, tiles_per_batch),
            _const_spec((1, d)),
            _const_spec(w_in.shape),
            _const_spec((1, d)),
            _const_spec(wmix.shape),
            _const_spec(bmix.shape),
        ],
        out_specs=[pl.BlockSpec((tm, d), lambda i: (i, 0)), v_spec],
        out_shape=[jax.ShapeDtypeStruct((t, d), BF16), jax.ShapeDtypeStruct(v_shape, F32)],
        compiler_params=_params(1),
        name="sgu",
    )(x, mods, norm_g.reshape(1, d), w_in, v_g.reshape(1, d), wmix, bmix)


def _ff_chunks(d_ff):
    step = 768
    return tuple((s, min(step, d_ff - s)) for s in range(0, d_ff, step))


def _ffn_kernel(x_ref, z_ref, mod_ref, wp_ref, g_ref, win_ref, wout_ref, o_ref):
    d = x_ref.shape[1]
    d_ff = wout_ref.shape[0]
    gate1 = mod_ref[:, 2 * d:3 * d]
    shift2 = mod_ref[:, 3 * d:4 * d]
    scale2 = mod_ref[:, 4 * d:5 * d]
    gate2 = mod_ref[:, 5 * d:6 * d]
    x1 = x_ref[...] + gate1 * _dot(z_ref[...].astype(BF16), wp_ref[...])
    h = _modulate(x1, g_ref[...], shift2, scale2).astype(BF16)
    acc = None
    for s, w in _ff_chunks(d_ff):
        a = _dot(h, win_ref[:, s:s + w])
        b = _dot(h, win_ref[:, d_ff + s:d_ff + s + w])
        act = (a * jax.nn.sigmoid(a) * b).astype(BF16)
        part = _dot(act, wout_ref[s:s + w, :])
        acc = part if acc is None else acc + part
    o_ref[...] = x1 + gate2 * acc


def _ffn(x, z, mods, tiles_per_batch, w_proj, norm_g, w_in, w_out):
    t, d = x.shape
    tm = TOKEN_TILE
    return pl.pallas_call(
        _ffn_kernel,
        grid=(t // tm,),
        in_specs=[
            pl.BlockSpec((tm, d), lambda i: (i, 0)),
            pl.BlockSpec((tm, d), lambda i: (i, 0)),
            _mod_spec(mods, tm, tiles_per_batch),
            _const_spec(w_proj.shape),
            _const_spec((1, d)),
            _const_spec(w_in.shape),
            _const_spec(w_out.shape),
        ],
        out_specs=pl.BlockSpec((tm, d), lambda i: (i, 0)),
        out_shape=jax.ShapeDtypeStruct((t, d), F32),
        compiler_params=_params(1),
        name="ffn",
    )(x, z, mods, w_proj, norm_g.reshape(1, d), w_in, w_out)


def _head_rms_t(xt, gain_col):
    d, tm = xt.shape
    x3 = xt.reshape(N_HEADS, d // N_HEADS, tm)
    x3 = x3 * lax.rsqrt(jnp.mean(x3 * x3, axis=1, keepdims=True) + EPS)
    return x3.reshape(d, tm) * gain_col


def _kvq_prompt_kernel(x_ref, kvmod_ref, mod_ref, kvg_ref, n1g_ref, wkt_ref, wvt_ref, wqt_ref, kg_ref, qg_ref,
                       kt_ref, vt_ref, kb_ref, vtb_ref, qt_ref, kmean_ref):
    tm, d = x_ref.shape
    x = x_ref[...]
    hkv = _modulate(x, kvg_ref[...], kvmod_ref[:, 0:d], kvmod_ref[:, d:2 * d]).astype(BF16)
    kt = _head_rms_t(_dot_nt(wkt_ref[...], hkv), kg_ref[...])
    vt = _dot_nt(wvt_ref[...], hkv)
    kt_ref[...] = kt
    vt_ref[...] = vt
    vtb_ref[...] = vt.astype(BF16)
    hq = _modulate(x, n1g_ref[...], mod_ref[:, 0:d], mod_ref[:, d:2 * d]).astype(BF16)
    qt_ref[...] = _head_rms_t(_dot_nt(wqt_ref[...], hq), qg_ref[...]).astype(BF16)
    for i in range(tm // MOBA_BLOCK):
        cols = slice(i * MOBA_BLOCK, (i + 1) * MOBA_BLOCK)
        k_nat = kt[:, cols].T
        kmean_ref[i] = jnp.mean(k_nat, axis=0, keepdims=True)
        kb_ref[cols, :] = k_nat.astype(BF16)


def _kvq_prompt(x, batch, kvmods, mods, kv_norm_g, norm1_g, w_kt, w_vt, w_qt, k_gain_col, q_gain_col):
    t, d = x.shape
    tm = TOKEN_TILE
    seq = t // batch
    tiles_per_batch = seq // tm
    nblk = tm // MOBA_BLOCK
    n_blocks = seq // MOBA_BLOCK
    row_spec = pl.BlockSpec((tm, d), lambda i: (i, 0))
    col_spec = pl.BlockSpec((None, d, tm), lambda i: (i // tiles_per_batch, 0, i % tiles_per_batch))
    return pl.pallas_call(
        _kvq_prompt_kernel,
        grid=(t // tm,),
        in_specs=[
            row_spec,
            _mod_spec(kvmods, tm, tiles_per_batch),
            _mod_spec(mods, tm, tiles_per_batch),
            _const_spec((1, d)), _const_spec((1, d)),
            _const_spec(w_kt.shape), _const_spec(w_vt.shape), _const_spec(w_qt.shape),
            _const_spec((d, 1)), _const_spec((d, 1)),
        ],
        out_specs=[
            col_spec, col_spec, row_spec, col_spec, col_spec,
            pl.BlockSpec((nblk, 1, d), lambda i: (i, 0, 0)),
        ],
        out_shape=[
            jax.ShapeDtypeStruct((batch, d, seq), F32),
            jax.ShapeDtypeStruct((batch, d, seq), F32),
            jax.ShapeDtypeStruct((t, d), BF16),
            jax.ShapeDtypeStruct((batch, d, seq), BF16),
            jax.ShapeDtypeStruct((batch, d, seq), BF16),
            jax.ShapeDtypeStruct((t // MOBA_BLOCK, 1, d), F32),
        ],
        compiler_params=_params(1),
        name="kvq_prompt",
    )(x, kvmods, mods, kv_norm_g.reshape(1, d), norm1_g.reshape(1, d), w_kt, w_vt, w_qt,
      k_gain_col, q_gain_col)


def _kvq_sample_kernel(x_ref, kvmod_ref, mod_ref, kvg_ref, n1g_ref, wk_ref, wv_ref, wq_ref, kg_ref, qg_ref,
                       pool_ref, expand_ref, k_ref, v_ref, q_ref):
    d = x_ref.shape[1]
    x = x_ref[...]
    hkv = _modulate(x, kvg_ref[...], kvmod_ref[:, 0:d], kvmod_ref[:, d:2 * d]).astype(BF16)
    k_ref[...] = _head_rms(_dot(hkv, wk_ref[...]), kg_ref[...], pool_ref, expand_ref)
    v_ref[...] = _dot(hkv, wv_ref[...])
    hq = _modulate(x, n1g_ref[...], mod_ref[:, 0:d], mod_ref[:, d:2 * d]).astype(BF16)
    q_ref[...] = _head_rms(_dot(hq, wq_ref[...]), qg_ref[...], pool_ref, expand_ref)


def _kvq_sample(x, kvmods, mods, kv_norm_g, norm1_g, w_k, w_v, w_q, k_gain, q_gain, pool, expand):
    t, d = x.shape
    tm = TOKEN_TILE
    row_spec = pl.BlockSpec((tm, d), lambda i: (i, 0))
    return pl.pallas_call(
        _kvq_sample_kernel,
        grid=(t // tm,),
        in_specs=[
            row_spec,
            _mod_spec(kvmods, tm, 1),
            _mod_spec(mods, tm, 1),
            _const_spec((1, d)), _const_spec((1, d)),
            _const_spec(w_k.shape), _const_spec(w_v.shape), _const_spec(w_q.shape),
            _const_spec((1, d)), _const_spec((1, d)),
            _const_spec(pool.shape), _const_spec(expand.shape),
        ],
        out_specs=[row_spec, row_spec, row_spec],
        out_shape=[jax.ShapeDtypeStruct((t, d), F32)] * 3,
        compiler_params=_params(1),
        name="kvq_sample",
    )(x, kvmods, mods, kv_norm_g.reshape(1, d), norm1_g.reshape(1, d), w_k, w_v, w_q,
      k_gain, q_gain, pool, expand)


def _top_k_additive_mask(gate, own, axis):
    nb = gate.shape[axis]
    idx = lax.broadcasted_iota(jnp.int32, gate.shape, axis)
    eligible = idx < own
    g = jnp.where(eligible, gate, NEG_INF)
    mask = jnp.full(gate.shape, NEG_INF, F32)
    for _ in range(min(MOBA_TOP_K, nb)):
        best = jnp.max(g, axis=axis, keepdims=True)
        first = jnp.min(jnp.where(g == best, idx, nb), axis=axis, keepdims=True)
        pick = idx == first
        mask = jnp.where(pick, 0.0, mask)
        g = jnp.where(pick, NEG_INF, g)
    return jnp.where(eligible, mask, NEG_INF)


def _attn_prompt_kernel(j, slopes_ref, qt_ref, kb_ref, vt_ref, kmean_ref, o_ref):
    group = pl.program_id(1)
    blk = MOBA_BLOCK
    pair_w = LANES
    n_heads = 2 * (qt_ref.shape[0] // pair_w)
    dh = pair_w // 2
    zeros = jnp.zeros((dh, blk), BF16)
    causal = (lax.broadcasted_iota(jnp.int32, (blk, blk), 0)
              <= lax.broadcasted_iota(jnp.int32, (blk, blk), 1))
    n_feat = 3
    k_feat = jnp.where(lax.broadcasted_iota(jnp.int32, (blk, LANES), 1) < n_feat,
                       lax.broadcasted_iota(jnp.int32, (blk, LANES), 0), 0).astype(F32).astype(BF16)
    feat_row = lax.broadcasted_iota(jnp.int32, (LANES, blk), 0)
    ones_rows = jnp.ones((16, blk), BF16)

    def prepare(h):
        lanes = slice((h // 2) * pair_w, (h // 2 + 1) * pair_w)
        slope = slopes_ref[group * n_heads + h]
        qt = qt_ref[lanes, :]
        qh = jnp.concatenate([qt[:dh], zeros] if h % 2 == 0 else [zeros, qt[dh:]], axis=0)
        q_feat = jnp.zeros((LANES, blk), F32)
        rest = jnp.full((1, blk), slope, F32)
        for f in range(n_feat):
            piece = rest.astype(BF16).astype(F32)
            q_feat = jnp.where(feat_row == f, piece, q_feat)
            rest = rest - piece
        q_aug = jnp.concatenate([qh, q_feat.astype(BF16)], axis=0)
        rows = []
        if j > 0:
            km_hi, km_lo = _split_bf16(kmean_ref[:, lanes])
            sel = _top_k_additive_mask(_dot(km_hi, qh) + _dot(km_lo, qh), j, 0)
            rows = [sel[i:i + 1, :] + slope * float((i - j) * blk) for i in range(j)]
        rows.append(jnp.zeros((1, blk), F32))
        return q_aug, rows

    def scores(h, q_aug, i):
        lanes = slice((h // 2) * pair_w, (h // 2 + 1) * pair_w)
        t = _dot(jnp.concatenate([kb_ref[i * blk:(i + 1) * blk, lanes], k_feat], axis=1), q_aug)
        return jnp.where(causal, t, NEG_INF) if i == j else t

    def values(h, t, shift, i):
        p = jnp.exp2(t - shift).astype(BF16)
        vt_aug = jnp.concatenate([vt_ref[h * dh:(h + 1) * dh, i * blk:(i + 1) * blk], ones_rows], axis=0)
        return _dot(vt_aug, p)

    outs = []
    prev = None
    for h in range(n_heads + 1):
        if h < n_heads:
            q_aug, rows = prepare(h)
            ts, m = [], None
        acc = None
        for i in range(j + 1):
            if h < n_heads:
                t = scores(h, q_aug, i)
                ts.append(t)
                mi = jnp.max(t, axis=0, keepdims=True) + rows[i]
                m = mi if m is None else jnp.maximum(m, mi)
            if prev is not None:
                p_ts, p_m, p_rows = prev
                part = values(h - 1, p_ts[i], p_m - p_rows[i], i)
                acc = part if acc is None else acc + part
        if prev is not None:
            outs.append(acc[:dh] / acc[dh:dh + 1])
        prev = (ts, m, rows) if h < n_heads else None
    o_ref[...] = jnp.concatenate(outs, axis=0).T.astype(o_ref.dtype)


ATTN_HEADS_PER_STEP = 8


def _attn_prompt(qt, kb, vt, kmean, slopes):
    batch, d, seq = qt.shape
    blk = MOBA_BLOCK
    n_blocks = seq // blk
    width = ATTN_HEADS_PER_STEP * (d // N_HEADS)
    outs = []
    for j in range(n_blocks):
        keys = (j + 1) * blk
        outs.append(pl.pallas_call(
            functools.partial(_attn_prompt_kernel, j),
            grid=(batch, d // width),
            in_specs=[
                pl.BlockSpec(memory_space=pltpu.SMEM),
                pl.BlockSpec((None, width, blk), lambda b, g, j=j: (b, g, j)),
                pl.BlockSpec((None, keys, width), lambda b, g: (b, 0, g)),
                pl.BlockSpec((None, width, keys), lambda b, g: (b, g, 0)),
                pl.BlockSpec((None, n_blocks, width), lambda b, g: (b, 0, g)),
            ],
            out_specs=pl.BlockSpec((None, blk, width), lambda b, g: (b, 0, g)),
            out_shape=jax.ShapeDtypeStruct((batch, blk, d), BF16),
            compiler_params=_params(2),
            name=f"attn_prompt_{j}",
        )(slopes, qt, kb, vt, kmean))
    return jnp.concatenate(outs, axis=1)


def _attn_sample_kernel(n_pages, pt_ref, q_ref, kn_ref, vn_ref, slope_ref, *refs):
    k_pages = refs[:n_pages]
    v_pages = refs[n_pages:2 * n_pages]
    o_ref = refs[2 * n_pages]
    nq, d = q_ref.shape
    dh = d // N_HEADS
    rows = nq * N_HEADS
    pages_per_block = MOBA_BLOCK // PAGE_SIZE
    n_past = n_pages // pages_per_block
    past_len = n_pages * PAGE_SIZE

    q = q_ref[...]
    lane_head = lax.broadcasted_iota(jnp.int32, (N_HEADS, d), 1) // dh
    head_mask = lane_head == lax.broadcasted_iota(jnp.int32, (N_HEADS, d), 0)
    qrows = jnp.concatenate([jnp.where(head_mask, q[i:i + 1, :], 0.0) for i in range(nq)], axis=0)
    qrows_b = qrows.astype(BF16)
    row_head_mask = jnp.concatenate([head_mask] * nq, axis=0)
    slope = slope_ref[...]
    row_q = lax.broadcasted_iota(jnp.int32, (rows, 1), 0) // N_HEADS

    scores = [_dot(qrows_b, k_pages[p][...].astype(BF16)) for p in range(n_pages)]
    gate = jnp.concatenate(
        [sum(jnp.sum(s, axis=1, keepdims=True) for s in scores[n * pages_per_block:(n + 1) * pages_per_block])
         for n in range(n_past)], axis=1) * (1.0 / MOBA_BLOCK)
    sel = _top_k_additive_mask(gate, n_past, 1)

    dist0 = (past_len + row_q).astype(F32)
    offs = lax.broadcasted_iota(jnp.int32, (1, PAGE_SIZE), 1)
    masked = []
    for p in range(n_pages):
        n = p // pages_per_block
        kpos = (p * PAGE_SIZE + offs).astype(F32)
        masked.append(scores[p] - slope * (dist0 - kpos) + sel[:, n:n + 1])
    kn = kn_ref[...]
    own = []
    for t in range(nq):
        s = jnp.sum(qrows * kn[t:t + 1, :], axis=1, keepdims=True)
        s = s - slope * (row_q - t).astype(F32)
        own.append(jnp.where(row_q >= t, s, NEG_INF))
    m = own[0]
    for s in own[1:]:
        m = jnp.maximum(m, s)
    for s in masked:
        m = jnp.maximum(m, jnp.max(s, axis=1, keepdims=True))
    l = jnp.zeros((rows, 1), F32)
    acc = jnp.zeros((rows, d), F32)
    vn = vn_ref[...]
    for t in range(nq):
        p_own = jnp.exp(own[t] - m)
        l = l + p_own
        acc = acc + p_own * vn[t:t + 1, :]
    for p in range(n_pages):
        pr = jnp.exp(masked[p] - m)
        l = l + jnp.sum(pr, axis=1, keepdims=True)
        acc = acc + _dot_nt(pr.astype(BF16), v_pages[p][...].astype(BF16))
    o_rows = jnp.where(row_head_mask, acc / l, 0.0)
    o_ref[...] = jnp.sum(o_rows.reshape(nq, N_HEADS, d), axis=1)


def _attn_sample(page_table, q, k_new, v_new, slope_rows, cache_k, cache_v):
    n_seq, nq, d = q.shape
    n_pages = page_table.shape[1]
    pool_pages = cache_k.shape[0]
    ck = cache_k.transpose(0, 2, 3, 1).reshape(pool_pages, d, PAGE_SIZE)
    cv = cache_v.transpose(0, 2, 3, 1).reshape(pool_pages, d, PAGE_SIZE)
    seq_spec = pl.BlockSpec((None, nq, d), lambda b, pt: (b, 0, 0))

    def page_spec(i):
        return pl.BlockSpec((None, d, PAGE_SIZE), lambda b, pt: (pt[b * n_pages + i], 0, 0))

    return pl.pallas_call(
        functools.partial(_attn_sample_kernel, n_pages),
        grid_spec=pltpu.PrefetchScalarGridSpec(
            num_scalar_prefetch=1,
            grid=(n_seq,),
            in_specs=[seq_spec, seq_spec, seq_spec,
                      pl.BlockSpec(slope_rows.shape, lambda b, pt: (0, 0))]
                     + [page_spec(i) for i in range(n_pages)] * 2,
            out_specs=seq_spec,
        ),
        out_shape=jax.ShapeDtypeStruct((n_seq, nq, d), F32),
        compiler_params=_params(1),
        name="attn_sample",
    )(page_table.reshape(-1), q, k_new, v_new, slope_rows, *([ck] * n_pages), *([cv] * n_pages))


def kernel(x_prompt, x_sample, c_prompt, c_sample, cache_k, cache_v, page_table, ada_w, ada_b, norm1_g, norm2_g,
           ffn_w_in, ffn_w_out, sgu_w_in, sgu_v_g, sgu_w_s, sgu_b_s, sgu_w_out, kv_ada_w, kv_ada_b, kv_norm_g,
           w_kv, k_norm_g, attn_w_q, q_norm_g, attn_w_o):
    batch, seq, d = x_prompt.shape
    n_seq, nq, _ = x_sample.shape
    dh = d // N_HEADS
    gd = d // N_SGU_GROUPS
    n_tok_p = batch * seq
    n_tok_s = n_seq * nq
    assert seq % TOKEN_TILE == 0 and n_tok_s % TOKEN_TILE == 0 and CHUNK % nq == 0

    n_c = batch + n_seq
    pad = (-n_c) % 8
    c_all = jnp.concatenate([c_prompt, c_sample, jnp.zeros((pad, d), F32)], axis=0)
    mods = _ada(c_all, ada_w, ada_b, 1536)
    kvmods = _ada(c_all, kv_ada_w[None], kv_ada_b[None], 1024)[0]
    mods_p = [mods[l, :batch][:, None, :] for l in range(2)]
    mods_s = [jnp.repeat(mods[l, batch:n_c], nq, axis=0) for l in range(2)]
    kvmods_p = kvmods[:batch][:, None, :]
    kvmods_s = jnp.repeat(kvmods[batch:n_c], nq, axis=0)

    bf = lambda w: w.astype(BF16)
    sgu_w_in_b, sgu_w_out_b = bf(sgu_w_in[0]), bf(sgu_w_out[0])
    ffn_in_b, ffn_out_b = bf(ffn_w_in), bf(ffn_w_out)
    w_k_b, w_v_b = bf(w_kv[:, :d]), bf(w_kv[:, d:])
    w_q_b, w_qt_b, w_o_b = bf(attn_w_q[0]), bf(attn_w_q[0].T), bf(attn_w_o[0])
    k_gain = jnp.tile(k_norm_g, N_HEADS).reshape(1, d)
    q_gain = (jnp.tile(q_norm_g[0], N_HEADS) * (dh ** -0.5)).reshape(1, d)
    head_of_lane = np.arange(d) // dh
    pool = jnp.asarray((head_of_lane[:, None] == np.arange(LANES)[None, :]) / dh, BF16)
    expand = jnp.asarray(np.arange(LANES)[:, None] == head_of_lane[None, :], BF16)
    slopes = jnp.exp2(-8.0 * jnp.arange(1, N_HEADS + 1, dtype=F32) / N_HEADS)

    wmix_p = sgu_w_s[0]
    bmix_p = jnp.repeat(sgu_b_s[0].T, gd, axis=1)
    wmix_s = jnp.tile(sgu_w_s[0][:, :nq, :nq], (1, CHUNK // nq, CHUNK // nq))
    bmix_s = jnp.tile(jnp.repeat(sgu_b_s[0][:, :nq].T, gd, axis=1), (CHUNK // nq, 1))

    xp = x_prompt.reshape(n_tok_p, d)
    xs = x_sample.reshape(n_tok_s, d)
    tiles_pb = seq // TOKEN_TILE

    uz_p, sguv_p = _sgu(xp, mods_p[0], tiles_pb, CHUNK, norm1_g[0], sgu_w_in_b, sgu_v_g[0], wmix_p, bmix_p,
                        CHUNK, (batch, CHUNK, d),
                        pl.BlockSpec((None, CHUNK, d), lambda i: (i // tiles_pb, 0, 0)))
    x2_p = _ffn(xp, uz_p, mods_p[0], tiles_pb, sgu_w_out_b, norm2_g[0], ffn_in_b[0], ffn_out_b[0])
    kt_p, vt_p, kb_p, vtb_p, qt_p, kmean_p = _kvq_prompt(
        x2_p, batch, kvmods_p, mods_p[1], kv_norm_g, norm1_g[1], bf(w_kv[:, :d].T), bf(w_kv[:, d:].T), w_qt_b,
        k_gain.reshape(d, 1), (q_gain * LOG2_E).reshape(d, 1))
    o_p = _attn_prompt(qt_p, kb_p.reshape(batch, seq, d), vtb_p,
                       kmean_p.reshape(batch, seq // MOBA_BLOCK, d), slopes * LOG2_E)
    y_p = _ffn(x2_p, o_p.reshape(n_tok_p, d), mods_p[1], tiles_pb, w_o_b, norm2_g[1], ffn_in_b[1], ffn_out_b[1])

    uz_s, sguv_s = _sgu(xs, mods_s[0], 1, nq, norm1_g[0], sgu_w_in_b, sgu_v_g[0], wmix_s, bmix_s,
                        TOKEN_TILE, (n_tok_s, d), pl.BlockSpec((TOKEN_TILE, d), lambda i: (i, 0)))
    x2_s = _ffn(xs, uz_s, mods_s[0], 1, sgu_w_out_b, norm2_g[0], ffn_in_b[0], ffn_out_b[0])
    k_s, v_s, q_s = _kvq_sample(x2_s, kvmods_s, mods_s[1], kv_norm_g, norm1_g[1], w_k_b, w_v_b, w_q_b,
                                k_gain, q_gain, pool, expand)
    slope_rows = jnp.tile(slopes, nq).reshape(nq * N_HEADS, 1)
    o_s = _attn_sample(page_table, q_s.reshape(n_seq, nq, d), k_s.reshape(n_seq, nq, d),
                       v_s.reshape(n_seq, nq, d), slope_rows, cache_k, cache_v)
    y_s = _ffn(x2_s, o_s.reshape(n_tok_s, d), mods_s[1], 1, w_o_b, norm2_g[1], ffn_in_b[1], ffn_out_b[1])

    return (y_p.reshape(batch, seq, d), y_s.reshape(n_seq, nq, d),
            kt_p.reshape(batch, N_HEADS, dh, seq).transpose(0, 3, 1, 2),
            vt_p.reshape(batch, N_HEADS, dh, seq).transpose(0, 3, 1, 2),
            k_s.reshape(n_seq, nq, N_HEADS, dh), v_s.reshape(n_seq, nq, N_HEADS, dh),
            sguv_p.reshape(1, batch, CHUNK, d), sguv_s.reshape(1, n_seq, nq, d))
```

```python
import functools

import numpy as np
import jax
import jax.numpy as jnp
from jax import lax
from jax.experimental import pallas as pl
from jax.experimental.pallas import tpu as pltpu

F32 = jnp.float32
BF16 = jnp.bfloat16
EPS = 1e-6
NEG_INF = float("-inf")
LOG2_E = 1.4426950408889634

V7X_VMEM_LIMIT_BYTES = 56 * 1024 * 1024
LANES = 128
SUBLANES = 8

N_HEADS = 16
MOBA_BLOCK = 256
MOBA_TOP_K = 3
CHUNK = 128
N_SGU_GROUPS = 8
PAGE_SIZE = 128
TOKEN_TILE = 512
ATTN_HEADS_PER_STEP = 8


def _params(n_grid_dims):
    return pltpu.CompilerParams(
        dimension_semantics=("arbitrary",) * n_grid_dims,
        vmem_limit_bytes=V7X_VMEM_LIMIT_BYTES)


def _const_spec(shape):
    n = len(shape)
    return pl.BlockSpec(shape, lambda *_: (0,) * n, pipeline_mode=pl.Buffered(1))


def _layer_spec(w, layer):
    tail = (0,) * (w.ndim - 1)
    return pl.BlockSpec((None,) + w.shape[1:], lambda *_: (layer,) + tail, pipeline_mode=pl.Buffered(1))


def _mod_spec(mods, layer, n_seq, tiles_per_batch):
    w = mods.shape[-1]
    if tiles_per_batch:
        return pl.BlockSpec((None, SUBLANES, w), lambda *_: (layer, n_seq // SUBLANES, 0))
    return pl.BlockSpec((None, n_seq, w), lambda *_: (layer, 0, 0))


def _mod_rows(mod_ref, tiles_per_batch):
    if tiles_per_batch:
        return mod_ref[pl.ds(pl.program_id(0) // tiles_per_batch, 1), :]
    return mod_ref[...]


def _per_token(v, tm):
    if v.shape[0] in (1, tm):
        return v
    return jnp.concatenate([v] * (tm // v.shape[0]), axis=0)


def _dot(a, b):
    return jnp.dot(a, b, preferred_element_type=F32)


def _dot_nt(a, b):
    return lax.dot_general(a, b, (((1,), (1,)), ((), ())), preferred_element_type=F32)


def _split_bf16(x):
    hi = x.astype(BF16)
    lo = (x - hi.astype(F32)).astype(BF16)
    return hi, lo


def _rms(x, g):
    return x * lax.rsqrt(jnp.mean(x * x, axis=-1, keepdims=True) + EPS) * g


def _modulate(x, g, shift, scale):
    tm = x.shape[0]
    return _rms(x, g) * (1.0 + _per_token(scale, tm)) + _per_token(shift, tm)


def _head_rms(x, gain, pool_ref, expand_ref):
    sq_hi, sq_lo = _split_bf16(x * x)
    pool = pool_ref[...]
    ms = _dot(sq_hi, pool) + _dot(sq_lo, pool)
    r_hi, r_lo = _split_bf16(lax.rsqrt(ms + EPS))
    expand = expand_ref[...]
    return x * (_dot(r_hi, expand) + _dot(r_lo, expand)) * gain


def _head_rms_t(xt, gain_col):
    d, tm = xt.shape
    x3 = xt.reshape(N_HEADS, d // N_HEADS, tm)
    x3 = x3 * lax.rsqrt(jnp.mean(x3 * x3, axis=1, keepdims=True) + EPS)
    return x3.reshape(d, tm) * gain_col


def _ada_kernel(c_ref, w_ref, b_ref, o_ref):
    c = c_ref[...]
    s = (c * jax.nn.sigmoid(c)).astype(BF16)
    o_ref[...] = _dot(s, w_ref[...].astype(BF16)) + b_ref[pl.ds(pl.program_id(0), 1), :]


def _ada(c_all, w, b, tn):
    n_layers, d, n = w.shape
    m = c_all.shape[0]
    return pl.pallas_call(
        _ada_kernel,
        grid=(n_layers, n // tn),
        in_specs=[
            pl.BlockSpec((m, d), lambda l, j: (0, 0)),
            pl.BlockSpec((None, d, tn), lambda l, j: (l, 0, j)),
            pl.BlockSpec((n_layers, tn), lambda l, j: (0, j)),
        ],
        out_specs=pl.BlockSpec((None, m, tn), lambda l, j: (l, 0, j)),
        out_shape=jax.ShapeDtypeStruct((n_layers, m, n), F32),
        compiler_params=_params(2),
        name="ada",
    )(c_all, w, b)


def _sgu_front(x_ref, mod, g_ref, win_ref, vg_ref):
    d = x_ref.shape[1]
    h = _modulate(x_ref[...], g_ref[0:1, :], mod[:, 0:d], mod[:, d:2 * d]).astype(BF16)
    uv = jax.nn.gelu(_dot(h, win_ref[...]))
    return uv[:, :d], _rms(uv[:, d:], vg_ref[...])


def _sgu_prompt_kernel(tiles_per_batch, x_ref, mod_ref, g_ref, win_ref, vg_ref, wmix_ref, bmix_ref, uz_ref, v_ref):
    tm, d = x_ref.shape
    c_rows = wmix_ref.shape[1]
    gd = d // N_SGU_GROUPS
    u, v = _sgu_front(x_ref, _mod_rows(mod_ref, tiles_per_batch), g_ref, win_ref, vg_ref)
    v_ref[...] = v[tm - v_ref.shape[0]:, :]
    vb = v.astype(BF16)
    causal = (lax.broadcasted_iota(jnp.int32, (c_rows, c_rows), 1)
              <= lax.broadcasted_iota(jnp.int32, (c_rows, c_rows), 0))
    for g in range(N_SGU_GROUPS):
        wg = jnp.where(causal, wmix_ref[g], 0.0).astype(BF16)
        cols = slice(g * gd, (g + 1) * gd)
        for ch in range(tm // c_rows):
            rows = slice(ch * c_rows, (ch + 1) * c_rows)
            mixed = _dot(wg, vb[rows, cols]) + bmix_ref[:, cols]
            uz_ref[rows, cols] = (u[rows, cols] * mixed).astype(BF16)


def _sgu_prompt(x, batch, mods, n_seq, norm_g, w_in, v_g, wmix, bmix):
    t, d = x.shape
    tm = TOKEN_TILE
    tiles_per_batch = t // batch // tm
    return pl.pallas_call(
        functools.partial(_sgu_prompt_kernel, tiles_per_batch),
        grid=(t // tm,),
        in_specs=[
            pl.BlockSpec((tm, d), lambda i: (i, 0)),
            _mod_spec(mods, 0, n_seq, tiles_per_batch),
            _const_spec(norm_g.shape),
            _const_spec(w_in.shape),
            _const_spec(v_g.shape),
            _const_spec(wmix.shape),
            _const_spec(bmix.shape),
        ],
        out_specs=[pl.BlockSpec((tm, d), lambda i: (i, 0)),
                   pl.BlockSpec((None, CHUNK, d), lambda i: (i // tiles_per_batch, 0, 0))],
        out_shape=[jax.ShapeDtypeStruct((t, d), BF16), jax.ShapeDtypeStruct((batch, CHUNK, d), F32)],
        compiler_params=_params(1),
        name="sgu_prompt",
    )(x, mods, norm_g, w_in, v_g, wmix, bmix)


def _sgu_sample_kernel(nq, x_ref, mod_ref, g_ref, win_ref, vg_ref, wcoef_ref, bcoef_ref, uz_ref, v_ref):
    tm = x_ref.shape[0]
    ns = tm // nq
    u, v = _sgu_front(x_ref, mod_ref[...], g_ref, win_ref, vg_ref)
    v_ref[...] = v
    for t in range(nq):
        mixed = bcoef_ref[t:t + 1, :]
        for s in range(t + 1):
            mixed = mixed + wcoef_ref[t * nq + s:t * nq + s + 1, :] * v[s * ns:(s + 1) * ns, :]
        uz_ref[t * ns:(t + 1) * ns, :] = (u[t * ns:(t + 1) * ns, :] * mixed).astype(BF16)


def _sgu_sample(x, nq, mods, n_seq, norm_g, w_in, v_g, wcoef, bcoef):
    t, d = x.shape
    full = pl.BlockSpec((t, d), lambda i: (0, 0))
    return pl.pallas_call(
        functools.partial(_sgu_sample_kernel, nq),
        grid=(1,),
        in_specs=[
            full,
            _mod_spec(mods, 0, n_seq, 0),
            _const_spec(norm_g.shape),
            _const_spec(w_in.shape),
            _const_spec(v_g.shape),
            _const_spec(wcoef.shape),
            _const_spec(bcoef.shape),
        ],
        out_specs=[full, full],
        out_shape=[jax.ShapeDtypeStruct((t, d), BF16), jax.ShapeDtypeStruct((t, d), F32)],
        compiler_params=_params(1),
        name="sgu_sample",
    )(x, mods, norm_g, w_in, v_g, wcoef, bcoef)


def _ff_chunks(d_ff):
    step = 768
    return tuple((s, min(step, d_ff - s)) for s in range(0, d_ff, step))


def _ffn_kernel(layer, tiles_per_batch, x_ref, z_ref, mod_ref, wp_ref, g_ref, win_ref, wout_ref, o_ref):
    tm, d = x_ref.shape
    d_ff = wout_ref.shape[0]
    mod = _mod_rows(mod_ref, tiles_per_batch)
    gate1 = _per_token(mod[:, 2 * d:3 * d], tm)
    gate2 = _per_token(mod[:, 5 * d:6 * d], tm)
    x1 = x_ref[...] + gate1 * _dot(z_ref[...].astype(BF16), wp_ref[...])
    h = _modulate(x1, g_ref[layer:layer + 1, :], mod[:, 3 * d:4 * d], mod[:, 4 * d:5 * d]).astype(BF16)
    acc = None
    for s, w in _ff_chunks(d_ff):
        a = _dot(h, win_ref[:, s:s + w])
        b = _dot(h, win_ref[:, d_ff + s:d_ff + s + w])
        act = (a * jax.nn.sigmoid(a) * b).astype(BF16)
        part = _dot(act, wout_ref[s:s + w, :])
        acc = part if acc is None else acc + part
    o_ref[...] = x1 + gate2 * acc


def _ffn(x, z, layer, mods, n_seq, tiles_per_batch, w_proj, norm_g, w_in, w_out):
    t, d = x.shape
    tm = TOKEN_TILE
    return pl.pallas_call(
        functools.partial(_ffn_kernel, layer, tiles_per_batch),
        grid=(t // tm,),
        in_specs=[
            pl.BlockSpec((tm, d), lambda i: (i, 0)),
            pl.BlockSpec((tm, d), lambda i: (i, 0)),
            _mod_spec(mods, layer, n_seq, tiles_per_batch),
            _const_spec(w_proj.shape),
            _const_spec(norm_g.shape),
            _layer_spec(w_in, layer),
            _layer_spec(w_out, layer),
        ],
        out_specs=pl.BlockSpec((tm, d), lambda i: (i, 0)),
        out_shape=jax.ShapeDtypeStruct((t, d), F32),
        compiler_params=_params(1),
        name="ffn",
    )(x, z, mods, w_proj, norm_g, w_in, w_out)


def _kvq_prompt_kernel(tiles_per_batch, x_ref, kvmod_ref, mod_ref, kvg_ref, n1g_ref, wkt_ref, wvt_ref, wqt_ref,
                       kg_ref, qg_ref, kt_ref, vt_ref, kb_ref, vtb_ref, qt_ref, kmean_ref):
    tm, d = x_ref.shape
    x = x_ref[...]
    kvmod = _mod_rows(kvmod_ref, tiles_per_batch)
    mod = _mod_rows(mod_ref, tiles_per_batch)
    hkv = _modulate(x, kvg_ref[...], kvmod[:, 0:d], kvmod[:, d:2 * d]).astype(BF16)
    kt = _head_rms_t(_dot_nt(wkt_ref[...], hkv), kg_ref[...])
    vt = _dot_nt(wvt_ref[...], hkv)
    kt_ref[...] = kt
    vt_ref[...] = vt
    vtb_ref[...] = vt.astype(BF16)
    hq = _modulate(x, n1g_ref[1:2, :], mod[:, 0:d], mod[:, d:2 * d]).astype(BF16)
    qt_ref[...] = _head_rms_t(_dot_nt(wqt_ref[...], hq), qg_ref[...]).astype(BF16)
    for i in range(tm // MOBA_BLOCK):
        cols = slice(i * MOBA_BLOCK, (i + 1) * MOBA_BLOCK)
        k_nat = kt[:, cols].T
        kmean_ref[i] = jnp.mean(k_nat, axis=0, keepdims=True)
        kb_ref[cols, :] = k_nat.astype(BF16)


def _kvq_prompt(x, batch, kvmods, mods, n_seq, kv_norm_g, norm1_g, w_kt, w_vt, w_qt, k_gain_col, q_gain_col):
    t, d = x.shape
    tm = TOKEN_TILE
    seq = t // batch
    tiles_per_batch = seq // tm
    nblk = tm // MOBA_BLOCK
    row_spec = pl.BlockSpec((tm, d), lambda i: (i, 0))
    col_spec = pl.BlockSpec((None, d, tm), lambda i: (i // tiles_per_batch, 0, i % tiles_per_batch))
    return pl.pallas_call(
        functools.partial(_kvq_prompt_kernel, tiles_per_batch),
        grid=(t // tm,),
        in_specs=[
            row_spec,
            _mod_spec(kvmods, 0, n_seq, tiles_per_batch),
            _mod_spec(mods, 1, n_seq, tiles_per_batch),
            _const_spec(kv_norm_g.shape), _const_spec(norm1_g.shape),
            _const_spec(w_kt.shape), _const_spec(w_vt.shape), _const_spec(w_qt.shape),
            _const_spec((d, 1)), _const_spec((d, 1)),
        ],
        out_specs=[
            col_spec, col_spec, row_spec, col_spec, col_spec,
            pl.BlockSpec((nblk, 1, d), lambda i: (i, 0, 0)),
        ],
        out_shape=[
            jax.ShapeDtypeStruct((batch, d, seq), F32),
            jax.ShapeDtypeStruct((batch, d, seq), F32),
            jax.ShapeDtypeStruct((t, d), BF16),
            jax.ShapeDtypeStruct((batch, d, seq), BF16),
            jax.ShapeDtypeStruct((batch, d, seq), BF16),
            jax.ShapeDtypeStruct((t // MOBA_BLOCK, 1, d), F32),
        ],
        compiler_params=_params(1),
        name="kvq_prompt",
    )(x, kvmods, mods, kv_norm_g, norm1_g, w_kt, w_vt, w_qt, k_gain_col, q_gain_col)


def _kvq_sample_kernel(nq, x_ref, kvmod_ref, mod_ref, kvg_ref, n1g_ref, wk_ref, wv_ref, wq_ref, kg_ref, qg_ref,
                       pool_ref, expand_ref, k_ref, v_ref, q_ref, kt_ref, vt_ref):
    tm, d = x_ref.shape
    ns = tm // nq
    x = x_ref[...]
    kvmod = kvmod_ref[...]
    mod = mod_ref[...]
    hkv = _modulate(x, kvg_ref[...], kvmod[:, 0:d], kvmod[:, d:2 * d]).astype(BF16)
    k = _head_rms(_dot(hkv, wk_ref[...]), kg_ref[...], pool_ref, expand_ref)
    v = _dot(hkv, wv_ref[...])
    k_ref[...] = k
    v_ref[...] = v
    for t in range(nq):
        kt_ref[t] = k[t * ns:(t + 1) * ns, :].T
        vt_ref[t] = v[t * ns:(t + 1) * ns, :].T
    hq = _modulate(x, n1g_ref[1:2, :], mod[:, 0:d], mod[:, d:2 * d]).astype(BF16)
    q_ref[...] = _head_rms(_dot(hq, wq_ref[...]), qg_ref[...], pool_ref, expand_ref)


def _kvq_sample(x, nq, kvmods, mods, n_seq, kv_norm_g, norm1_g, w_k, w_v, w_q, k_gain, q_gain, pool, expand):
    t, d = x.shape
    full = pl.BlockSpec((t, d), lambda i: (0, 0))
    tspec = pl.BlockSpec((nq, d, t // nq), lambda i: (0, 0, 0))
    return pl.pallas_call(
        functools.partial(_kvq_sample_kernel, nq),
        grid=(1,),
        in_specs=[
            full,
            _mod_spec(kvmods, 0, n_seq, 0),
            _mod_spec(mods, 1, n_seq, 0),
            _const_spec(kv_norm_g.shape), _const_spec(norm1_g.shape),
            _const_spec(w_k.shape), _const_spec(w_v.shape), _const_spec(w_q.shape),
            _const_spec((1, d)), _const_spec((1, d)),
            _const_spec(pool.shape), _const_spec(expand.shape),
        ],
        out_specs=[full, full, full, tspec, tspec],
        out_shape=[jax.ShapeDtypeStruct((t, d), F32)] * 3 + [jax.ShapeDtypeStruct((nq, d, t // nq), F32)] * 2,
        compiler_params=_params(1),
        name="kvq_sample",
    )(x, kvmods, mods, kv_norm_g, norm1_g, w_k, w_v, w_q, k_gain, q_gain, pool, expand)


def _top_k_additive_mask(gate, own, axis):
    nb = gate.shape[axis]
    idx = lax.broadcasted_iota(jnp.int32, gate.shape, axis)
    eligible = idx < own
    g = jnp.where(eligible, gate, NEG_INF)
    mask = jnp.full(gate.shape, NEG_INF, F32)
    for _ in range(min(MOBA_TOP_K, nb)):
        best = jnp.max(g, axis=axis, keepdims=True)
        first = jnp.min(jnp.where(g == best, idx, nb), axis=axis, keepdims=True)
        pick = idx == first
        mask = jnp.where(pick, 0.0, mask)
        g = jnp.where(pick, NEG_INF, g)
    return jnp.where(eligible, mask, NEG_INF)


def _attn_prompt_kernel(j, slopes_ref, qt_ref, kb_ref, vt_ref, kmean_ref, *rest):
    o_ref = rest[-1]
    group = pl.program_id(1)
    blk = MOBA_BLOCK
    pair_w = LANES
    n_heads = 2 * (qt_ref.shape[0] // pair_w)
    dh = pair_w // 2
    zeros = jnp.zeros((dh, blk), BF16)
    causal = (lax.broadcasted_iota(jnp.int32, (blk, blk), 0)
              <= lax.broadcasted_iota(jnp.int32, (blk, blk), 1))
    n_feat = 3
    k_feat = jnp.where(lax.broadcasted_iota(jnp.int32, (blk, LANES), 1) < n_feat,
                       lax.broadcasted_iota(jnp.int32, (blk, LANES), 0), 0).astype(F32).astype(BF16)
    feat_row = lax.broadcasted_iota(jnp.int32, (LANES, blk), 0)
    ones_rows = jnp.ones((16, blk), BF16)

    def prepare(h):
        lanes = slice((h // 2) * pair_w, (h // 2 + 1) * pair_w)
        slope = slopes_ref[group * n_heads + h]
        qt = qt_ref[lanes, :]
        qh = jnp.concatenate([qt[:dh], zeros] if h % 2 == 0 else [zeros, qt[dh:]], axis=0)
        q_feat = jnp.zeros((LANES, blk), F32)
        rest_slope = jnp.full((1, blk), slope, F32)
        for f in range(n_feat):
            piece = rest_slope.astype(BF16).astype(F32)
            q_feat = jnp.where(feat_row == f, piece, q_feat)
            rest_slope = rest_slope - piece
        q_aug = jnp.concatenate([qh, q_feat.astype(BF16)], axis=0)
        rows = []
        if j > 0:
            km_hi, km_lo = _split_bf16(kmean_ref[:, lanes])
            sel = _top_k_additive_mask(_dot(km_hi, qh) + _dot(km_lo, qh), j, 0)
            rows = [sel[i:i + 1, :] + slope * float((i - j) * blk) for i in range(j)]
        rows.append(jnp.zeros((1, blk), F32))
        return q_aug, rows

    def scores(h, q_aug, i):
        lanes = slice((h // 2) * pair_w, (h // 2 + 1) * pair_w)
        t = _dot(jnp.concatenate([kb_ref[i * blk:(i + 1) * blk, lanes], k_feat], axis=1), q_aug)
        return jnp.where(causal, t, NEG_INF) if i == j else t

    def values(h, t, shift, i):
        p = jnp.exp2((t - shift).astype(BF16))
        vt_aug = jnp.concatenate([vt_ref[h * dh:(h + 1) * dh, i * blk:(i + 1) * blk], ones_rows], axis=0)
        return _dot(vt_aug, p)

    outs = []
    prev = None
    for h in range(n_heads + 1):
        if h < n_heads:
            q_aug, rows = prepare(h)
            ts, m = [], None
        acc = None
        for i in range(j + 1):
            if h < n_heads:
                t = scores(h, q_aug, i)
                ts.append(t)
                mi = jnp.max(t, axis=0, keepdims=True) + rows[i]
                m = mi if m is None else jnp.maximum(m, mi)
            if prev is not None:
                p_ts, p_m, p_rows = prev
                part = values(h - 1, p_ts[i], p_m - p_rows[i], i)
                acc = part if acc is None else acc + part
        if prev is not None:
            outs.append(acc[:dh] / acc[dh:dh + 1])
        prev = (ts, m, rows) if h < n_heads else None
    o_ref[...] = jnp.concatenate(outs, axis=0).T.astype(o_ref.dtype)


def _attn_prompt(qt, kb, vt, kmean, slopes):
    batch, d, seq = qt.shape
    blk = MOBA_BLOCK
    n_blocks = seq // blk
    width = ATTN_HEADS_PER_STEP * (d // N_HEADS)
    out = None
    for j in range(n_blocks):
        keys = (j + 1) * blk
        in_specs = [
            pl.BlockSpec(memory_space=pltpu.SMEM),
            pl.BlockSpec((None, width, blk), lambda b, g, j=j: (b, g, j)),
            pl.BlockSpec((None, keys, width), lambda b, g: (b, 0, g)),
            pl.BlockSpec((None, width, keys), lambda b, g: (b, g, 0)),
            pl.BlockSpec((None, n_blocks, width), lambda b, g: (b, 0, g)),
        ]
        args = [slopes, qt, kb, vt, kmean]
        aliases = {}
        if out is not None:
            in_specs.append(pl.BlockSpec(memory_space=pl.ANY))
            args.append(out)
            aliases = {len(args) - 1: 0}
        out = pl.pallas_call(
            functools.partial(_attn_prompt_kernel, j),
            grid=(batch, d // width),
            in_specs=in_specs,
            out_specs=pl.BlockSpec((None, blk, width), lambda b, g, j=j: (b, j, g)),
            out_shape=jax.ShapeDtypeStruct((batch, seq, d), BF16),
            input_output_aliases=aliases,
            compiler_params=_params(2),
            name=f"attn_prompt_{j}",
        )(*args)
    return out


def _attn_sample_kernel(n_pages, pt_ref, q_ref, kn_ref, vn_ref, slope_ref, *refs):
    k_pages = refs[:n_pages]
    v_pages = refs[n_pages:2 * n_pages]
    o_ref = refs[2 * n_pages]
    nq, _, d = q_ref.shape
    dh = d // N_HEADS
    rows = nq * N_HEADS
    pages_per_block = MOBA_BLOCK // PAGE_SIZE
    n_past = n_pages // pages_per_block
    past_len = n_pages * PAGE_SIZE

    lane_head = lax.broadcasted_iota(jnp.int32, (N_HEADS, d), 1) // dh
    head_mask = lane_head == lax.broadcasted_iota(jnp.int32, (N_HEADS, d), 0)
    qrows = jnp.concatenate([jnp.where(head_mask, q_ref[i], 0.0) for i in range(nq)], axis=0)
    qrows_b = qrows.astype(BF16)
    row_head_mask = jnp.concatenate([head_mask] * nq, axis=0)
    slope = slope_ref[...]
    row_q = lax.broadcasted_iota(jnp.int32, (rows, 1), 0) // N_HEADS

    scores = [_dot(qrows_b, k_pages[p][...].astype(BF16)) for p in range(n_pages)]
    gate = jnp.concatenate(
        [sum(jnp.sum(s, axis=1, keepdims=True) for s in scores[n * pages_per_block:(n + 1) * pages_per_block])
         for n in range(n_past)], axis=1) * (1.0 / MOBA_BLOCK)
    sel = _top_k_additive_mask(gate, n_past, 1)

    dist0 = (past_len + row_q).astype(F32)
    offs = lax.broadcasted_iota(jnp.int32, (1, PAGE_SIZE), 1)
    masked = []
    for p in range(n_pages):
        n = p // pages_per_block
        kpos = (p * PAGE_SIZE + offs).astype(F32)
        masked.append(scores[p] - slope * (dist0 - kpos) + sel[:, n:n + 1])
    own = []
    for t in range(nq):
        s = jnp.sum(qrows * kn_ref[t], axis=1, keepdims=True)
        s = s - slope * (row_q - t).astype(F32)
        own.append(jnp.where(row_q >= t, s, NEG_INF))
    m = own[0]
    for s in own[1:]:
        m = jnp.maximum(m, s)
    for s in masked:
        m = jnp.maximum(m, jnp.max(s, axis=1, keepdims=True))
    l = jnp.zeros((rows, 1), F32)
    acc = jnp.zeros((rows, d), F32)
    for t in range(nq):
        p_own = jnp.exp(own[t] - m)
        l = l + p_own
        acc = acc + p_own * vn_ref[t]
    for p in range(n_pages):
        pr = jnp.exp(masked[p] - m)
        l = l + jnp.sum(pr, axis=1, keepdims=True)
        acc = acc + _dot_nt(pr.astype(BF16), v_pages[p][...].astype(BF16))
    o_rows = jnp.where(row_head_mask, acc / l, 0.0)
    o = jnp.sum(o_rows.reshape(nq, N_HEADS, d), axis=1)
    for t in range(nq):
        o_ref[t] = o[t:t + 1, :]


def _attn_sample(page_table, q, k_new, v_new, slope_rows, cache_k, cache_v):
    nq, n_seq, d = q.shape
    n_pages = page_table.shape[1]
    pool_pages = cache_k.shape[0]
    ck = cache_k.transpose(0, 2, 3, 1).reshape(pool_pages, d, PAGE_SIZE)
    cv = cache_v.transpose(0, 2, 3, 1).reshape(pool_pages, d, PAGE_SIZE)
    seq_spec = pl.BlockSpec((nq, None, 1, d), lambda b, pt: (0, b, 0, 0))
    per_seq = lambda a: a.reshape(nq, n_seq, 1, d)

    def page_spec(i):
        return pl.BlockSpec((None, d, PAGE_SIZE), lambda b, pt: (pt[b * n_pages + i], 0, 0))

    return pl.pallas_call(
        functools.partial(_attn_sample_kernel, n_pages),
        grid_spec=pltpu.PrefetchScalarGridSpec(
            num_scalar_prefetch=1,
            grid=(n_seq,),
            in_specs=[seq_spec, seq_spec, seq_spec,
                      pl.BlockSpec(slope_rows.shape, lambda b, pt: (0, 0))]
                     + [page_spec(i) for i in range(n_pages)] * 2,
            out_specs=seq_spec,
        ),
        out_shape=jax.ShapeDtypeStruct((nq, n_seq, 1, d), F32),
        compiler_params=_params(1),
        name="attn_sample",
    )(page_table.reshape(-1), per_seq(q), per_seq(k_new), per_seq(v_new), slope_rows, *([ck] * n_pages), *([cv] * n_pages))


def kernel(x_prompt, x_sample, c_prompt, c_sample, cache_k, cache_v, page_table, ada_w, ada_b, norm1_g, norm2_g,
           ffn_w_in, ffn_w_out, sgu_w_in, sgu_v_g, sgu_w_s, sgu_b_s, sgu_w_out, kv_ada_w, kv_ada_b, kv_norm_g,
           w_kv, k_norm_g, attn_w_q, q_norm_g, attn_w_o):
    batch, seq, d = x_prompt.shape
    n_seq, nq, _ = x_sample.shape
    dh = d // N_HEADS
    gd = d // N_SGU_GROUPS
    n_tok_p = batch * seq
    n_tok_s = n_seq * nq
    assert seq % TOKEN_TILE == 0 and n_tok_s == TOKEN_TILE and n_seq % SUBLANES == 0 and batch <= SUBLANES

    c_all = jnp.concatenate([c_sample, c_prompt, jnp.zeros((SUBLANES - batch, d), F32)], axis=0)
    mods = _ada(c_all, ada_w, ada_b, 1536)
    kvmods = _ada(c_all, kv_ada_w[None], kv_ada_b[None], 1024)

    bf = lambda w: w.astype(BF16)
    sgu_w_in_b, sgu_w_out_b = bf(sgu_w_in[0]), bf(sgu_w_out[0])
    ffn_in_b, ffn_out_b = bf(ffn_w_in), bf(ffn_w_out)
    w_k_b, w_v_b = bf(w_kv[:, :d]), bf(w_kv[:, d:])
    w_q_b, w_qt_b, w_o_b = bf(attn_w_q[0]), bf(attn_w_q[0].T), bf(attn_w_o[0])
    k_gain = jnp.tile(k_norm_g, N_HEADS).reshape(1, d)
    q_gain = (jnp.tile(q_norm_g[0], N_HEADS) * (dh ** -0.5)).reshape(1, d)
    kv_norm_row = kv_norm_g.reshape(1, d)
    head_of_lane = np.arange(d) // dh
    pool = jnp.asarray((head_of_lane[:, None] == np.arange(LANES)[None, :]) / dh, BF16)
    expand = jnp.asarray(np.arange(LANES)[:, None] == head_of_lane[None, :], BF16)
    slopes = jnp.exp2(-8.0 * jnp.arange(1, N_HEADS + 1, dtype=F32) / N_HEADS)

    wmix_p = sgu_w_s[0]
    bmix_p = jnp.repeat(sgu_b_s[0].T, gd, axis=1)
    wcoef_s = jnp.repeat(sgu_w_s[0][:, :nq, :nq].transpose(1, 2, 0).reshape(nq * nq, N_SGU_GROUPS), gd, axis=1)
    bcoef_s = jnp.repeat(sgu_b_s[0][:, :nq].T, gd, axis=1)

    xp = x_prompt.reshape(n_tok_p, d)
    xs = x_sample.transpose(1, 0, 2).reshape(n_tok_s, d)
    tiles_pb = seq // TOKEN_TILE

    uz_p, sguv_p = _sgu_prompt(xp, batch, mods, n_seq, norm1_g, sgu_w_in_b, sgu_v_g, wmix_p, bmix_p)
    x2_p = _ffn(xp, uz_p, 0, mods, n_seq, tiles_pb, sgu_w_out_b, norm2_g, ffn_in_b, ffn_out_b)
    kt_p, vt_p, kb_p, vtb_p, qt_p, kmean_p = _kvq_prompt(
        x2_p, batch, kvmods, mods, n_seq, kv_norm_row, norm1_g, bf(w_kv[:, :d].T), bf(w_kv[:, d:].T), w_qt_b,
        k_gain.reshape(d, 1), (q_gain * LOG2_E).reshape(d, 1))
    o_p = _attn_prompt(qt_p, kb_p.reshape(batch, seq, d), vtb_p,
                       kmean_p.reshape(batch, seq // MOBA_BLOCK, d), slopes * LOG2_E)
    y_p = _ffn(x2_p, o_p.reshape(n_tok_p, d), 1, mods, n_seq, tiles_pb, w_o_b, norm2_g, ffn_in_b, ffn_out_b)

    uz_s, sguv_s = _sgu_sample(xs, nq, mods, n_seq, norm1_g, sgu_w_in_b, sgu_v_g, wcoef_s, bcoef_s)
    x2_s = _ffn(xs, uz_s, 0, mods, n_seq, 0, sgu_w_out_b, norm2_g, ffn_in_b, ffn_out_b)
    k_s, v_s, q_s, kt_s, vt_s = _kvq_sample(x2_s, nq, kvmods, mods, n_seq, kv_norm_row, norm1_g, w_k_b, w_v_b,
                                            w_q_b, k_gain, q_gain, pool, expand)
    slope_rows = jnp.tile(slopes, nq).reshape(nq * N_HEADS, 1)
    o_s = _attn_sample(page_table, q_s.reshape(nq, n_seq, d), k_s.reshape(nq, n_seq, d),
                       v_s.reshape(nq, n_seq, d), slope_rows, cache_k, cache_v)
    y_s = _ffn(x2_s, o_s.reshape(n_tok_s, d), 1, mods, n_seq, 0, w_o_b, norm2_g, ffn_in_b, ffn_out_b)

    by_seq = lambda a: a.reshape(nq, n_seq, d).transpose(1, 0, 2)
    return (y_p.reshape(batch, seq, d), by_seq(y_s),
            kt_p.reshape(batch, N_HEADS, dh, seq).transpose(0, 3, 1, 2),
            vt_p.reshape(batch, N_HEADS, dh, seq).transpose(0, 3, 1, 2),
            kt_s.reshape(nq, N_HEADS, dh, n_seq).transpose(3, 0, 1, 2),
            vt_s.reshape(nq, N_HEADS, dh, n_seq).transpose(3, 0, 1, 2),
            sguv_p.reshape(1, batch, CHUNK, d), by_seq(sguv_s)[None])
```

```python
import functools

import numpy as np
import jax
import jax.numpy as jnp
from jax import lax
from jax.experimental import pallas as pl
from jax.experimental.pallas import tpu as pltpu

F32 = jnp.float32
BF16 = jnp.bfloat16
EPS = 1e-6
NEG_INF = float("-inf")
LOG2_E = 1.4426950408889634

V7X_VMEM_LIMIT_BYTES = 56 * 1024 * 1024
LANES = 128
SUBLANES = 8

N_HEADS = 16
MOBA_BLOCK = 256
MOBA_TOP_K = 3
CHUNK = 128
N_SGU_GROUPS = 8
PAGE_SIZE = 128
TOKEN_TILE = 512
ATTN_HEADS_PER_STEP = 8


def _params(n_grid_dims):
    return pltpu.CompilerParams(
        dimension_semantics=("arbitrary",) * n_grid_dims,
        vmem_limit_bytes=V7X_VMEM_LIMIT_BYTES)


def _const_spec(shape):
    n = len(shape)
    return pl.BlockSpec(shape, lambda *_: (0,) * n, pipeline_mode=pl.Buffered(1))


def _layer_spec(w, layer):
    tail = (0,) * (w.ndim - 1)
    return pl.BlockSpec((None,) + w.shape[1:], lambda *_: (layer,) + tail, pipeline_mode=pl.Buffered(1))


def _mod_spec(mods, layer, n_seq, tiles_per_batch):
    w = mods.shape[-1]
    if tiles_per_batch:
        return pl.BlockSpec((None, SUBLANES, w), lambda *_: (layer, n_seq // SUBLANES, 0))
    return pl.BlockSpec((None, n_seq, w), lambda *_: (layer, 0, 0))


def _mod_rows(mod_ref, tiles_per_batch):
    if tiles_per_batch:
        return mod_ref[pl.ds(pl.program_id(0) // tiles_per_batch, 1), :]
    return mod_ref[...]


def _per_token(v, tm):
    if v.shape[0] in (1, tm):
        return v
    return jnp.concatenate([v] * (tm // v.shape[0]), axis=0)


def _dot(a, b):
    return jnp.dot(a, b, preferred_element_type=F32)


def _dot_nt(a, b):
    return lax.dot_general(a, b, (((1,), (1,)), ((), ())), preferred_element_type=F32)


def _split_bf16(x):
    hi = x.astype(BF16)
    lo = (x - hi.astype(F32)).astype(BF16)
    return hi, lo


def _rms(x, g):
    return x * lax.rsqrt(jnp.mean(x * x, axis=-1, keepdims=True) + EPS) * g


def _modulate(x, g, shift, scale):
    tm = x.shape[0]
    return _rms(x, g) * (1.0 + _per_token(scale, tm)) + _per_token(shift, tm)


def _head_rms(x, gain, pool_ref, expand_ref):
    sq_hi, sq_lo = _split_bf16(x * x)
    pool = pool_ref[...]
    ms = _dot(sq_hi, pool) + _dot(sq_lo, pool)
    r_hi, r_lo = _split_bf16(lax.rsqrt(ms + EPS))
    expand = expand_ref[...]
    return x * (_dot(r_hi, expand) + _dot(r_lo, expand)) * gain


def _head_rms_t(xt, gain_col):
    d, tm = xt.shape
    x3 = xt.reshape(N_HEADS, d // N_HEADS, tm)
    x3 = x3 * lax.rsqrt(jnp.mean(x3 * x3, axis=1, keepdims=True) + EPS)
    return x3.reshape(d, tm) * gain_col


def _ada_kernel(c_ref, w_ref, b_ref, o_ref):
    c = c_ref[...]
    s = (c * jax.nn.sigmoid(c)).astype(BF16)
    o_ref[...] = _dot(s, w_ref[...].astype(BF16)) + b_ref[pl.ds(pl.program_id(0), 1), :]


def _ada(c_all, w, b, tn):
    n_layers, d, n = w.shape
    m = c_all.shape[0]
    return pl.pallas_call(
        _ada_kernel,
        grid=(n_layers, n // tn),
        in_specs=[
            pl.BlockSpec((m, d), lambda l, j: (0, 0)),
            pl.BlockSpec((None, d, tn), lambda l, j: (l, 0, j)),
            pl.BlockSpec((n_layers, tn), lambda l, j: (0, j)),
        ],
        out_specs=pl.BlockSpec((None, m, tn), lambda l, j: (l, 0, j)),
        out_shape=jax.ShapeDtypeStruct((n_layers, m, n), F32),
        compiler_params=_params(2),
        name="ada",
    )(c_all, w, b)


def _sgu_front(x_ref, mod, g_ref, win_ref, vg_ref):
    d = x_ref.shape[1]
    h = _modulate(x_ref[...], g_ref[0:1, :], mod[:, 0:d], mod[:, d:2 * d]).astype(BF16)
    uv = jax.nn.gelu(_dot(h, win_ref[...]))
    return uv[:, :d], _rms(uv[:, d:], vg_ref[...])


def _sgu_prompt_kernel(tiles_per_batch, x_ref, mod_ref, g_ref, win_ref, vg_ref, wmix_ref, bmix_ref, uz_ref, v_ref):
    tm, d = x_ref.shape
    c_rows = wmix_ref.shape[1]
    gd = d // N_SGU_GROUPS
    u, v = _sgu_front(x_ref, _mod_rows(mod_ref, tiles_per_batch), g_ref, win_ref, vg_ref)
    v_ref[...] = v[tm - v_ref.shape[0]:, :]
    vb = v.astype(BF16)
    causal = (lax.broadcasted_iota(jnp.int32, (c_rows, c_rows), 1)
              <= lax.broadcasted_iota(jnp.int32, (c_rows, c_rows), 0))
    for g in range(N_SGU_GROUPS):
        wg = jnp.where(causal, wmix_ref[g], 0.0).astype(BF16)
        cols = slice(g * gd, (g + 1) * gd)
        for ch in range(tm // c_rows):
            rows = slice(ch * c_rows, (ch + 1) * c_rows)
            mixed = _dot(wg, vb[rows, cols]) + bmix_ref[:, cols]
            uz_ref[rows, cols] = (u[rows, cols] * mixed).astype(BF16)


def _sgu_prompt(x, batch, mods, n_seq, norm_g, w_in, v_g, wmix, bmix):
    t, d = x.shape
    tm = TOKEN_TILE
    tiles_per_batch = t // batch // tm
    return pl.pallas_call(
        functools.partial(_sgu_prompt_kernel, tiles_per_batch),
        grid=(t // tm,),
        in_specs=[
            pl.BlockSpec((tm, d), lambda i: (i, 0)),
            _mod_spec(mods, 0, n_seq, tiles_per_batch),
            _const_spec(norm_g.shape),
            _const_spec(w_in.shape),
            _const_spec(v_g.shape),
            _const_spec(wmix.shape),
            _const_spec(bmix.shape),
        ],
        out_specs=[pl.BlockSpec((tm, d), lambda i: (i, 0)),
                   pl.BlockSpec((None, CHUNK, d), lambda i: (i // tiles_per_batch, 0, 0))],
        out_shape=[jax.ShapeDtypeStruct((t, d), BF16), jax.ShapeDtypeStruct((batch, CHUNK, d), F32)],
        compiler_params=_params(1),
        name="sgu_prompt",
    )(x, mods, norm_g, w_in, v_g, wmix, bmix)


def _sgu_sample_kernel(nq, x_ref, mod_ref, g_ref, win_ref, vg_ref, wcoef_ref, bcoef_ref, uz_ref, v_ref):
    tm = x_ref.shape[0]
    ns = tm // nq
    u, v = _sgu_front(x_ref, mod_ref[...], g_ref, win_ref, vg_ref)
    v_ref[...] = v
    for t in range(nq):
        mixed = bcoef_ref[t:t + 1, :]
        for s in range(t + 1):
            mixed = mixed + wcoef_ref[t * nq + s:t * nq + s + 1, :] * v[s * ns:(s + 1) * ns, :]
        uz_ref[t * ns:(t + 1) * ns, :] = (u[t * ns:(t + 1) * ns, :] * mixed).astype(BF16)


def _sgu_sample(x, nq, mods, n_seq, norm_g, w_in, v_g, wcoef, bcoef):
    t, d = x.shape
    full = pl.BlockSpec((t, d), lambda i: (0, 0))
    return pl.pallas_call(
        functools.partial(_sgu_sample_kernel, nq),
        grid=(1,),
        in_specs=[
            full,
            _mod_spec(mods, 0, n_seq, 0),
            _const_spec(norm_g.shape),
            _const_spec(w_in.shape),
            _const_spec(v_g.shape),
            _const_spec(wcoef.shape),
            _const_spec(bcoef.shape),
        ],
        out_specs=[full, full],
        out_shape=[jax.ShapeDtypeStruct((t, d), BF16), jax.ShapeDtypeStruct((t, d), F32)],
        compiler_params=_params(1),
        name="sgu_sample",
    )(x, mods, norm_g, w_in, v_g, wcoef, bcoef)


def _ff_chunks(d_ff):
    step = 768
    return tuple((s, min(step, d_ff - s)) for s in range(0, d_ff, step))


def _ffn_kernel(layer, tiles_per_batch, x_ref, z_ref, mod_ref, wp_ref, g_ref, win_ref, wout_ref, o_ref):
    tm, d = x_ref.shape
    d_ff = wout_ref.shape[0]
    mod = _mod_rows(mod_ref, tiles_per_batch)
    gate1 = _per_token(mod[:, 2 * d:3 * d], tm)
    gate2 = _per_token(mod[:, 5 * d:6 * d], tm)
    x1 = x_ref[...] + gate1 * _dot(z_ref[...].astype(BF16), wp_ref[...])
    h = _modulate(x1, g_ref[layer:layer + 1, :], mod[:, 3 * d:4 * d], mod[:, 4 * d:5 * d]).astype(BF16)
    acc = None
    for s, w in _ff_chunks(d_ff):
        a = _dot(h, win_ref[:, s:s + w])
        b = _dot(h, win_ref[:, d_ff + s:d_ff + s + w])
        act = (a * jax.nn.sigmoid(a) * b).astype(BF16)
        part = _dot(act, wout_ref[s:s + w, :])
        acc = part if acc is None else acc + part
    o_ref[...] = x1 + gate2 * acc


def _ffn(x, z, layer, mods, n_seq, tiles_per_batch, w_proj, norm_g, w_in, w_out):
    t, d = x.shape
    tm = TOKEN_TILE
    return pl.pallas_call(
        functools.partial(_ffn_kernel, layer, tiles_per_batch),
        grid=(t // tm,),
        in_specs=[
            pl.BlockSpec((tm, d), lambda i: (i, 0)),
            pl.BlockSpec((tm, d), lambda i: (i, 0)),
            _mod_spec(mods, layer, n_seq, tiles_per_batch),
            _const_spec(w_proj.shape),
            _const_spec(norm_g.shape),
            _layer_spec(w_in, layer),
            _layer_spec(w_out, layer),
        ],
        out_specs=pl.BlockSpec((tm, d), lambda i: (i, 0)),
        out_shape=jax.ShapeDtypeStruct((t, d), F32),
        compiler_params=_params(1),
        name="ffn",
    )(x, z, mods, w_proj, norm_g, w_in, w_out)


def _kvq_prompt_kernel(tiles_per_batch, x_ref, kvmod_ref, mod_ref, kvg_ref, n1g_ref, wkt_ref, wvt_ref, wqt_ref,
                       kg_ref, qg_ref, kt_ref, vt_ref, kb_ref, vtb_ref, qt_ref, kmean_ref):
    tm, d = x_ref.shape
    x = x_ref[...]
    kvmod = _mod_rows(kvmod_ref, tiles_per_batch)
    mod = _mod_rows(mod_ref, tiles_per_batch)
    hkv = _modulate(x, kvg_ref[...], kvmod[:, 0:d], kvmod[:, d:2 * d]).astype(BF16)
    kt = _head_rms_t(_dot_nt(wkt_ref[...], hkv), kg_ref[...])
    vt = _dot_nt(wvt_ref[...], hkv)
    kt_ref[...] = kt
    vt_ref[...] = vt
    vtb_ref[...] = vt.astype(BF16)
    hq = _modulate(x, n1g_ref[1:2, :], mod[:, 0:d], mod[:, d:2 * d]).astype(BF16)
    qt_ref[...] = _head_rms_t(_dot_nt(wqt_ref[...], hq), qg_ref[...]).astype(BF16)
    for i in range(tm // MOBA_BLOCK):
        cols = slice(i * MOBA_BLOCK, (i + 1) * MOBA_BLOCK)
        k_nat = kt[:, cols].T
        kmean_ref[i] = jnp.mean(k_nat, axis=0, keepdims=True)
        kb_ref[cols, :] = k_nat.astype(BF16)


def _kvq_prompt(x, batch, kvmods, mods, n_seq, kv_norm_g, norm1_g, w_kt, w_vt, w_qt, k_gain_col, q_gain_col):
    t, d = x.shape
    tm = TOKEN_TILE
    seq = t // batch
    tiles_per_batch = seq // tm
    nblk = tm // MOBA_BLOCK
    row_spec = pl.BlockSpec((tm, d), lambda i: (i, 0))
    col_spec = pl.BlockSpec((None, d, tm), lambda i: (i // tiles_per_batch, 0, i % tiles_per_batch))
    return pl.pallas_call(
        functools.partial(_kvq_prompt_kernel, tiles_per_batch),
        grid=(t // tm,),
        in_specs=[
            row_spec,
            _mod_spec(kvmods, 0, n_seq, tiles_per_batch),
            _mod_spec(mods, 1, n_seq, tiles_per_batch),
            _const_spec(kv_norm_g.shape), _const_spec(norm1_g.shape),
            _const_spec(w_kt.shape), _const_spec(w_vt.shape), _const_spec(w_qt.shape),
            _const_spec((d, 1)), _const_spec((d, 1)),
        ],
        out_specs=[
            col_spec, col_spec, row_spec, col_spec, col_spec,
            pl.BlockSpec((nblk, 1, d), lambda i: (i, 0, 0)),
        ],
        out_shape=[
            jax.ShapeDtypeStruct((batch, d, seq), F32),
            jax.ShapeDtypeStruct((batch, d, seq), F32),
            jax.ShapeDtypeStruct((t, d), BF16),
            jax.ShapeDtypeStruct((batch, d, seq), BF16),
            jax.ShapeDtypeStruct((batch, d, seq), BF16),
            jax.ShapeDtypeStruct((t // MOBA_BLOCK, 1, d), F32),
        ],
        compiler_params=_params(1),
        name="kvq_prompt",
    )(x, kvmods, mods, kv_norm_g, norm1_g, w_kt, w_vt, w_qt, k_gain_col, q_gain_col)


def _kvq_sample_kernel(nq, x_ref, kvmod_ref, mod_ref, kvg_ref, n1g_ref, wk_ref, wv_ref, wq_ref, kg_ref, qg_ref,
                       pool_ref, expand_ref, k_ref, v_ref, q_ref, kt_ref, vt_ref):
    tm, d = x_ref.shape
    ns = tm // nq
    x = x_ref[...]
    kvmod = kvmod_ref[...]
    mod = mod_ref[...]
    hkv = _modulate(x, kvg_ref[...], kvmod[:, 0:d], kvmod[:, d:2 * d]).astype(BF16)
    k = _head_rms(_dot(hkv, wk_ref[...]), kg_ref[...], pool_ref, expand_ref)
    v = _dot(hkv, wv_ref[...])
    k_ref[...] = k
    v_ref[...] = v
    for t in range(nq):
        kt_ref[t] = k[t * ns:(t + 1) * ns, :].T
        vt_ref[t] = v[t * ns:(t + 1) * ns, :].T
    hq = _modulate(x, n1g_ref[1:2, :], mod[:, 0:d], mod[:, d:2 * d]).astype(BF16)
    q_ref[...] = _head_rms(_dot(hq, wq_ref[...]), qg_ref[...], pool_ref, expand_ref)


def _kvq_sample(x, nq, kvmods, mods, n_seq, kv_norm_g, norm1_g, w_k, w_v, w_q, k_gain, q_gain, pool, expand):
    t, d = x.shape
    full = pl.BlockSpec((t, d), lambda i: (0, 0))
    tspec = pl.BlockSpec((nq, d, t // nq), lambda i: (0, 0, 0))
    return pl.pallas_call(
        functools.partial(_kvq_sample_kernel, nq),
        grid=(1,),
        in_specs=[
            full,
            _mod_spec(kvmods, 0, n_seq, 0),
            _mod_spec(mods, 1, n_seq, 0),
            _const_spec(kv_norm_g.shape), _const_spec(norm1_g.shape),
            _const_spec(w_k.shape), _const_spec(w_v.shape), _const_spec(w_q.shape),
            _const_spec((1, d)), _const_spec((1, d)),
            _const_spec(pool.shape), _const_spec(expand.shape),
        ],
        out_specs=[full, full, full, tspec, tspec],
        out_shape=[jax.ShapeDtypeStruct((t, d), F32)] * 3 + [jax.ShapeDtypeStruct((nq, d, t // nq), F32)] * 2,
        compiler_params=_params(1),
        name="kvq_sample",
    )(x, kvmods, mods, kv_norm_g, norm1_g, w_k, w_v, w_q, k_gain, q_gain, pool, expand)


def _top_k_additive_mask(gate, own, axis):
    nb = gate.shape[axis]
    idx = lax.broadcasted_iota(jnp.int32, gate.shape, axis)
    eligible = idx < own
    g = jnp.where(eligible, gate, NEG_INF)
    mask = jnp.full(gate.shape, NEG_INF, F32)
    for _ in range(min(MOBA_TOP_K, nb)):
        best = jnp.max(g, axis=axis, keepdims=True)
        first = jnp.min(jnp.where(g == best, idx, nb), axis=axis, keepdims=True)
        pick = idx == first
        mask = jnp.where(pick, 0.0, mask)
        g = jnp.where(pick, NEG_INF, g)
    return jnp.where(eligible, mask, NEG_INF)


def _prompt_attention(j, slopes_ref, qt_ref, kb_ref, vt_ref, kmean_ref, o_ref, hooks):
    group = pl.program_id(1)
    blk = MOBA_BLOCK
    pair_w = LANES
    n_heads = 2 * (qt_ref.shape[0] // pair_w)
    dh = pair_w // 2
    zeros = jnp.zeros((dh, blk), BF16)
    causal = (lax.broadcasted_iota(jnp.int32, (blk, blk), 0)
              <= lax.broadcasted_iota(jnp.int32, (blk, blk), 1))
    n_feat = 3
    k_feat = jnp.where(lax.broadcasted_iota(jnp.int32, (blk, LANES), 1) < n_feat,
                       lax.broadcasted_iota(jnp.int32, (blk, LANES), 0), 0).astype(F32).astype(BF16)
    feat_row = lax.broadcasted_iota(jnp.int32, (LANES, blk), 0)
    ones_rows = jnp.ones((16, blk), BF16)

    def prepare(h):
        lanes = slice((h // 2) * pair_w, (h // 2 + 1) * pair_w)
        slope = slopes_ref[group * n_heads + h]
        qt = qt_ref[lanes, :]
        qh = jnp.concatenate([qt[:dh], zeros] if h % 2 == 0 else [zeros, qt[dh:]], axis=0)
        q_feat = jnp.zeros((LANES, blk), F32)
        rest_slope = jnp.full((1, blk), slope, F32)
        for f in range(n_feat):
            piece = rest_slope.astype(BF16).astype(F32)
            q_feat = jnp.where(feat_row == f, piece, q_feat)
            rest_slope = rest_slope - piece
        q_aug = jnp.concatenate([qh, q_feat.astype(BF16)], axis=0)
        rows = []
        if j > 0:
            km_hi, km_lo = _split_bf16(kmean_ref[:, lanes])
            sel = _top_k_additive_mask(_dot(km_hi, qh) + _dot(km_lo, qh), j, 0)
            rows = [sel[i:i + 1, :] + slope * float((i - j) * blk) for i in range(j)]
        rows.append(jnp.zeros((1, blk), F32))
        return q_aug, rows

    def scores(h, q_aug, i):
        lanes = slice((h // 2) * pair_w, (h // 2 + 1) * pair_w)
        t = _dot(jnp.concatenate([kb_ref[i * blk:(i + 1) * blk, lanes], k_feat], axis=1), q_aug)
        return jnp.where(causal, t, NEG_INF) if i == j else t

    def values(h, t, shift, i):
        p = jnp.exp2((t - shift).astype(BF16))
        vt_aug = jnp.concatenate([vt_ref[h * dh:(h + 1) * dh, i * blk:(i + 1) * blk], ones_rows], axis=0)
        return _dot(vt_aug, p)

    outs = []
    prev = None
    for h in range(n_heads + 1):
        if h in hooks:
            hooks[h]()
        if h < n_heads:
            q_aug, rows = prepare(h)
            ts, m = [], None
        acc = None
        for i in range(j + 1):
            if h < n_heads:
                t = scores(h, q_aug, i)
                ts.append(t)
                mi = jnp.max(t, axis=0, keepdims=True) + rows[i]
                m = mi if m is None else jnp.maximum(m, mi)
            if prev is not None:
                p_ts, p_m, p_rows = prev
                part = values(h - 1, p_ts[i], p_m - p_rows[i], i)
                acc = part if acc is None else acc + part
        if prev is not None:
            outs.append(acc[:dh] / acc[dh:dh + 1])
        prev = (ts, m, rows) if h < n_heads else None
    o_ref[...] = jnp.concatenate(outs, axis=0).T.astype(o_ref.dtype)


def _sample_attention(n_pages, r, q_ref, kn_ref, vn_ref, slope_ref, kbuf, vbuf, o_ref):
    nq, _, _, d = q_ref.shape
    dh = d // N_HEADS
    rows = nq * N_HEADS
    pages_per_block = MOBA_BLOCK // PAGE_SIZE
    n_past = n_pages // pages_per_block
    past_len = n_pages * PAGE_SIZE

    lane_head = lax.broadcasted_iota(jnp.int32, (N_HEADS, d), 1) // dh
    head_mask = lane_head == lax.broadcasted_iota(jnp.int32, (N_HEADS, d), 0)
    qrows = jnp.concatenate([jnp.where(head_mask, q_ref[i, r], 0.0) for i in range(nq)], axis=0)
    qrows_b = qrows.astype(BF16)
    row_head_mask = jnp.concatenate([head_mask] * nq, axis=0)
    slope = slope_ref[...]
    row_q = lax.broadcasted_iota(jnp.int32, (rows, 1), 0) // N_HEADS

    scores = [_dot(qrows_b, kbuf[p].astype(BF16)) for p in range(n_pages)]
    gate = jnp.concatenate(
        [sum(jnp.sum(s, axis=1, keepdims=True) for s in scores[n * pages_per_block:(n + 1) * pages_per_block])
         for n in range(n_past)], axis=1) * (1.0 / MOBA_BLOCK)
    sel = _top_k_additive_mask(gate, n_past, 1)

    dist0 = (past_len + row_q).astype(F32)
    offs = lax.broadcasted_iota(jnp.int32, (1, PAGE_SIZE), 1)
    masked = []
    for p in range(n_pages):
        n = p // pages_per_block
        kpos = (p * PAGE_SIZE + offs).astype(F32)
        masked.append(scores[p] - slope * (dist0 - kpos) + sel[:, n:n + 1])
    own = []
    for t in range(nq):
        s = jnp.sum(qrows * kn_ref[t, r], axis=1, keepdims=True)
        s = s - slope * (row_q - t).astype(F32)
        own.append(jnp.where(row_q >= t, s, NEG_INF))
    m = own[0]
    for s in own[1:]:
        m = jnp.maximum(m, s)
    for s in masked:
        m = jnp.maximum(m, jnp.max(s, axis=1, keepdims=True))
    l = jnp.zeros((rows, 1), F32)
    acc = jnp.zeros((rows, d), F32)
    for t in range(nq):
        p_own = jnp.exp(own[t] - m)
        l = l + p_own
        acc = acc + p_own * vn_ref[t, r]
    for p in range(n_pages):
        pr = jnp.exp(masked[p] - m)
        l = l + jnp.sum(pr, axis=1, keepdims=True)
        acc = acc + _dot_nt(pr.astype(BF16), vbuf[p].astype(BF16))
    o_rows = jnp.where(row_head_mask, acc / l, 0.0)
    o = jnp.sum(o_rows.reshape(nq, N_HEADS, d), axis=1)
    for t in range(nq):
        o_ref[t, r] = o[t:t + 1, :]


def _attn_kernel(j, n_slots, seq_base, n_pages, pt_ref, slopes_ref, qt_ref, kb_ref, vt_ref, kmean_ref,
                 o_in_ref, *rest):
    if not n_slots:
        (o_ref,) = rest
        _prompt_attention(j, slopes_ref, qt_ref, kb_ref, vt_ref, kmean_ref, o_ref, {})
        return
    qs_ref, kn_ref, vn_ref, srow_ref, ck_ref, cv_ref, os_in_ref, o_ref, os_ref, kbuf, vbuf, sem = rest
    step = pl.program_id(0) * pl.num_programs(1) + pl.program_id(1)

    def page_copies(r):
        seq = seq_base + step * n_slots + r
        copies = []
        for i in range(n_pages):
            page = pt_ref[seq * n_pages + i]
            copies.append(pltpu.make_async_copy(ck_ref.at[page], kbuf.at[i], sem.at[0, i]))
            copies.append(pltpu.make_async_copy(cv_ref.at[page], vbuf.at[i], sem.at[1, i]))
        return copies

    def start(r):
        for c in page_copies(r):
            c.start()

    def finish(r):
        for c in page_copies(r):
            c.wait()
        _sample_attention(n_pages, r, qs_ref, kn_ref, vn_ref, srow_ref, kbuf, vbuf, os_ref)

    n_stages = 2 * (qt_ref.shape[0] // LANES) + 1
    hooks = {0: functools.partial(start, 0)}
    for r in range(1, n_slots):
        def swap(r=r):
            finish(r - 1)
            start(r)
        hooks[(r * n_stages) // n_slots] = swap
    _prompt_attention(j, slopes_ref, qt_ref, kb_ref, vt_ref, kmean_ref, o_ref, hooks)
    finish(n_slots - 1)


def _sample_slots(n_blocks, steps, n_seq, max_slots=2):
    assert n_seq % steps == 0
    left = n_seq // steps
    slots = [0] * n_blocks
    for j in reversed(range(n_blocks)):
        slots[j] = min(max_slots, left)
        left -= slots[j]
    assert left == 0
    return slots


def _attention(qt, kb, vt, kmean, slopes, page_table, q_s, k_s, v_s, slope_rows, cache_k, cache_v):
    batch, d, seq = qt.shape
    nq, n_seq, _ = q_s.shape
    blk = MOBA_BLOCK
    n_blocks = seq // blk
    width = ATTN_HEADS_PER_STEP * (d // N_HEADS)
    groups = d // width
    n_pages = page_table.shape[1]
    pool_pages = cache_k.shape[0]
    ck = cache_k.transpose(0, 2, 3, 1).reshape(pool_pages, d, PAGE_SIZE)
    cv = cache_v.transpose(0, 2, 3, 1).reshape(pool_pages, d, PAGE_SIZE)
    per_seq = lambda a: a.reshape(nq, n_seq, 1, d)
    slots = _sample_slots(n_blocks, batch * groups, n_seq)
    pt = page_table.reshape(-1)
    out = jnp.zeros((batch, seq, d), BF16)
    out_s = jnp.zeros((nq, n_seq, 1, d), F32)
    seq_base = 0
    for j in range(n_blocks):
        keys = (j + 1) * blk
        n_slots = slots[j]
        in_specs = [
            pl.BlockSpec(memory_space=pltpu.SMEM),
            pl.BlockSpec((None, width, blk), lambda b, g, pt, j=j: (b, g, j)),
            pl.BlockSpec((None, keys, width), lambda b, g, pt: (b, 0, g)),
            pl.BlockSpec((None, width, keys), lambda b, g, pt: (b, g, 0)),
            pl.BlockSpec((None, n_blocks, width), lambda b, g, pt: (b, 0, g)),
            pl.BlockSpec(memory_space=pl.ANY),
        ]
        args = [pt, slopes, qt, kb, vt, kmean, out]
        out_specs = [pl.BlockSpec((None, blk, width), lambda b, g, pt, j=j: (b, j, g))]
        out_shape = [jax.ShapeDtypeStruct((batch, seq, d), BF16)]
        scratch = []
        aliases = {len(args) - 1: 0}
        if n_slots:
            first_block = seq_base // n_slots
            assert seq_base % n_slots == 0
            slot_spec = pl.BlockSpec((nq, n_slots, 1, d),
                                     lambda b, g, pt, f=first_block: (0, f + b * groups + g, 0, 0))
            in_specs += [slot_spec, slot_spec, slot_spec,
                         pl.BlockSpec(slope_rows.shape, lambda b, g, pt: (0, 0)),
                         pl.BlockSpec(memory_space=pl.ANY), pl.BlockSpec(memory_space=pl.ANY),
                         pl.BlockSpec(memory_space=pl.ANY)]
            args += [per_seq(q_s), per_seq(k_s), per_seq(v_s), slope_rows, ck, cv, out_s]
            aliases[len(args) - 1] = 1
            out_specs.append(slot_spec)
            out_shape.append(jax.ShapeDtypeStruct((nq, n_seq, 1, d), F32))
            scratch = [pltpu.VMEM((n_pages, d, PAGE_SIZE), F32), pltpu.VMEM((n_pages, d, PAGE_SIZE), F32),
                       pltpu.SemaphoreType.DMA((2, n_pages))]
        res = pl.pallas_call(
            functools.partial(_attn_kernel, j, n_slots, seq_base, n_pages),
            grid_spec=pltpu.PrefetchScalarGridSpec(
                num_scalar_prefetch=1, grid=(batch, groups), in_specs=in_specs, out_specs=out_specs,
                scratch_shapes=scratch),
            out_shape=out_shape,
            input_output_aliases=aliases,
            compiler_params=_params(2),
            name=f"attn_{j}",
        )(*args)
        out = res[0]
        if n_slots:
            out_s = res[1]
            seq_base += n_slots * batch * groups
    return out, out_s


def kernel(x_prompt, x_sample, c_prompt, c_sample, cache_k, cache_v, page_table, ada_w, ada_b, norm1_g, norm2_g,
           ffn_w_in, ffn_w_out, sgu_w_in, sgu_v_g, sgu_w_s, sgu_b_s, sgu_w_out, kv_ada_w, kv_ada_b, kv_norm_g,
           w_kv, k_norm_g, attn_w_q, q_norm_g, attn_w_o):
    batch, seq, d = x_prompt.shape
    n_seq, nq, _ = x_sample.shape
    dh = d // N_HEADS
    gd = d // N_SGU_GROUPS
    n_tok_p = batch * seq
    n_tok_s = n_seq * nq
    assert seq % TOKEN_TILE == 0 and n_tok_s == TOKEN_TILE and n_seq % SUBLANES == 0 and batch <= SUBLANES

    c_all = jnp.concatenate([c_sample, c_prompt, jnp.zeros((SUBLANES - batch, d), F32)], axis=0)
    mods = _ada(c_all, ada_w, ada_b, 1536)
    kvmods = _ada(c_all, kv_ada_w[None], kv_ada_b[None], 1024)

    bf = lambda w: w.astype(BF16)
    sgu_w_in_b, sgu_w_out_b = bf(sgu_w_in[0]), bf(sgu_w_out[0])
    ffn_in_b, ffn_out_b = bf(ffn_w_in), bf(ffn_w_out)
    w_k_b, w_v_b = bf(w_kv[:, :d]), bf(w_kv[:, d:])
    w_q_b, w_qt_b, w_o_b = bf(attn_w_q[0]), bf(attn_w_q[0].T), bf(attn_w_o[0])
    k_gain = jnp.tile(k_norm_g, N_HEADS).reshape(1, d)
    q_gain = (jnp.tile(q_norm_g[0], N_HEADS) * (dh ** -0.5)).reshape(1, d)
    kv_norm_row = kv_norm_g.reshape(1, d)
    head_of_lane = np.arange(d) // dh
    pool = jnp.asarray((head_of_lane[:, None] == np.arange(LANES)[None, :]) / dh, BF16)
    expand = jnp.asarray(np.arange(LANES)[:, None] == head_of_lane[None, :], BF16)
    slopes = jnp.exp2(-8.0 * jnp.arange(1, N_HEADS + 1, dtype=F32) / N_HEADS)

    wmix_p = sgu_w_s[0]
    bmix_p = jnp.repeat(sgu_b_s[0].T, gd, axis=1)
    wcoef_s = jnp.repeat(sgu_w_s[0][:, :nq, :nq].transpose(1, 2, 0).reshape(nq * nq, N_SGU_GROUPS), gd, axis=1)
    bcoef_s = jnp.repeat(sgu_b_s[0][:, :nq].T, gd, axis=1)

    xp = x_prompt.reshape(n_tok_p, d)
    xs = x_sample.transpose(1, 0, 2).reshape(n_tok_s, d)
    tiles_pb = seq // TOKEN_TILE

    uz_p, sguv_p = _sgu_prompt(xp, batch, mods, n_seq, norm1_g, sgu_w_in_b, sgu_v_g, wmix_p, bmix_p)
    x2_p = _ffn(xp, uz_p, 0, mods, n_seq, tiles_pb, sgu_w_out_b, norm2_g, ffn_in_b, ffn_out_b)
    kt_p, vt_p, kb_p, vtb_p, qt_p, kmean_p = _kvq_prompt(
        x2_p, batch, kvmods, mods, n_seq, kv_norm_row, norm1_g, bf(w_kv[:, :d].T), bf(w_kv[:, d:].T), w_qt_b,
        k_gain.reshape(d, 1), (q_gain * LOG2_E).reshape(d, 1))
    uz_s, sguv_s = _sgu_sample(xs, nq, mods, n_seq, norm1_g, sgu_w_in_b, sgu_v_g, wcoef_s, bcoef_s)
    x2_s = _ffn(xs, uz_s, 0, mods, n_seq, 0, sgu_w_out_b, norm2_g, ffn_in_b, ffn_out_b)
    k_s, v_s, q_s, kt_s, vt_s = _kvq_sample(x2_s, nq, kvmods, mods, n_seq, kv_norm_row, norm1_g, w_k_b, w_v_b,
                                            w_q_b, k_gain, q_gain, pool, expand)

    slope_rows = jnp.tile(slopes, nq).reshape(nq * N_HEADS, 1)
    o_p, o_s = _attention(qt_p, kb_p.reshape(batch, seq, d), vtb_p, kmean_p.reshape(batch, seq // MOBA_BLOCK, d),
                          slopes * LOG2_E, page_table, q_s.reshape(nq, n_seq, d), k_s.reshape(nq, n_seq, d),
                          v_s.reshape(nq, n_seq, d), slope_rows, cache_k, cache_v)
    y_p = _ffn(x2_p, o_p.reshape(n_tok_p, d), 1, mods, n_seq, tiles_pb, w_o_b, norm2_g, ffn_in_b, ffn_out_b)
    y_s = _ffn(x2_s, o_s.reshape(n_tok_s, d), 1, mods, n_seq, 0, w_o_b, norm2_g, ffn_in_b, ffn_out_b)

    by_seq = lambda a: a.reshape(nq, n_seq, d).transpose(1, 0, 2)
    return (y_p.reshape(batch, seq, d), by_seq(y_s),
            kt_p.reshape(batch, N_HEADS, dh, seq).transpose(0, 3, 1, 2),
            vt_p.reshape(batch, N_HEADS, dh, seq).transpose(0, 3, 1, 2),
            kt_s.reshape(nq, N_HEADS, dh, n_seq).transpose(3, 0, 1, 2),
            vt_s.reshape(nq, N_HEADS, dh, n_seq).transpose(3, 0, 1, 2),
            sguv_p.reshape(1, batch, CHUNK, d), by_seq(sguv_s)[None])
```

```python
import functools

import numpy as np
import jax
import jax.numpy as jnp
from jax import lax
from jax.experimental import pallas as pl
from jax.experimental.pallas import tpu as pltpu

F32 = jnp.float32
BF16 = jnp.bfloat16
EPS = 1e-6
NEG_INF = float("-inf")
LOG2_E = 1.4426950408889634

V7X_VMEM_LIMIT_BYTES = 56 * 1024 * 1024
LANES = 128
SUBLANES = 8

N_HEADS = 16
MOBA_BLOCK = 256
MOBA_TOP_K = 3
CHUNK = 128
N_SGU_GROUPS = 8
PAGE_SIZE = 128
TOKEN_TILE = 512
ATTN_HEADS_PER_STEP = 8


def _params(n_grid_dims):
    return pltpu.CompilerParams(
        dimension_semantics=("arbitrary",) * n_grid_dims,
        vmem_limit_bytes=V7X_VMEM_LIMIT_BYTES)


def _const_spec(shape):
    n = len(shape)
    return pl.BlockSpec(shape, lambda *_: (0,) * n, pipeline_mode=pl.Buffered(1))


def _layer_spec(w, layer):
    tail = (0,) * (w.ndim - 1)
    return pl.BlockSpec((None,) + w.shape[1:], lambda *_: (layer,) + tail, pipeline_mode=pl.Buffered(1))


def _mod_spec(mods, layer, n_seq, tiles_per_batch):
    w = mods.shape[-1]
    if tiles_per_batch:
        return pl.BlockSpec((None, SUBLANES, w), lambda *_: (layer, n_seq // SUBLANES, 0))
    return pl.BlockSpec((None, n_seq, w), lambda *_: (layer, 0, 0))


def _mod_rows(mod_ref, tiles_per_batch):
    if tiles_per_batch:
        return mod_ref[pl.ds(pl.program_id(0) // tiles_per_batch, 1), :]
    return mod_ref[...]


def _per_token(v, tm):
    if v.shape[0] in (1, tm):
        return v
    return jnp.concatenate([v] * (tm // v.shape[0]), axis=0)


def _dot(a, b):
    return jnp.dot(a, b, preferred_element_type=F32)


def _dot_nt(a, b):
    return lax.dot_general(a, b, (((1,), (1,)), ((), ())), preferred_element_type=F32)


def _split_bf16(x):
    hi = x.astype(BF16)
    lo = (x - hi.astype(F32)).astype(BF16)
    return hi, lo


def _rms(x, g):
    return x * lax.rsqrt(jnp.mean(x * x, axis=-1, keepdims=True) + EPS) * g


def _modulate(x, g, shift, scale):
    tm = x.shape[0]
    return _rms(x, g) * (1.0 + _per_token(scale, tm)) + _per_token(shift, tm)


def _head_rms(x, gain, pool_ref, expand_ref):
    sq_hi, sq_lo = _split_bf16(x * x)
    pool = pool_ref[...]
    ms = _dot(sq_hi, pool) + _dot(sq_lo, pool)
    r_hi, r_lo = _split_bf16(lax.rsqrt(ms + EPS))
    expand = expand_ref[...]
    return x * (_dot(r_hi, expand) + _dot(r_lo, expand)) * gain


def _head_rms_t(xt, gain_col):
    d, tm = xt.shape
    x3 = xt.reshape(N_HEADS, d // N_HEADS, tm)
    x3 = x3 * lax.rsqrt(jnp.mean(x3 * x3, axis=1, keepdims=True) + EPS)
    return x3.reshape(d, tm) * gain_col


def _ada_kernel(c_ref, w_ref, b_ref, o_ref):
    c = c_ref[...]
    s = (c * jax.nn.sigmoid(c)).astype(BF16)
    o_ref[...] = _dot(s, w_ref[...].astype(BF16)) + b_ref[pl.ds(pl.program_id(0), 1), :]


def _ada(c_all, w, b, tn):
    n_layers, d, n = w.shape
    m = c_all.shape[0]
    return pl.pallas_call(
        _ada_kernel,
        grid=(n_layers, n // tn),
        in_specs=[
            pl.BlockSpec((m, d), lambda l, j: (0, 0)),
            pl.BlockSpec((None, d, tn), lambda l, j: (l, 0, j)),
            pl.BlockSpec((n_layers, tn), lambda l, j: (0, j)),
        ],
        out_specs=pl.BlockSpec((None, m, tn), lambda l, j: (l, 0, j)),
        out_shape=jax.ShapeDtypeStruct((n_layers, m, n), F32),
        compiler_params=_params(2),
        name="ada",
    )(c_all, w, b)


def _sgu_front(x_ref, mod, g_ref, win_ref, vg_ref):
    d = x_ref.shape[1]
    h = _modulate(x_ref[...], g_ref[0:1, :], mod[:, 0:d], mod[:, d:2 * d]).astype(BF16)
    uv = jax.nn.gelu(_dot(h, win_ref[...]))
    return uv[:, :d], _rms(uv[:, d:], vg_ref[...])


def _sgu_prompt_kernel(tiles_per_batch, x_ref, mod_ref, g_ref, win_ref, vg_ref, wmix_ref, bmix_ref, uz_ref, v_ref):
    tm, d = x_ref.shape
    c_rows = wmix_ref.shape[1]
    gd = d // N_SGU_GROUPS
    u, v = _sgu_front(x_ref, _mod_rows(mod_ref, tiles_per_batch), g_ref, win_ref, vg_ref)
    v_ref[...] = v[tm - v_ref.shape[0]:, :]
    vb = v.astype(BF16)
    causal = (lax.broadcasted_iota(jnp.int32, (c_rows, c_rows), 1)
              <= lax.broadcasted_iota(jnp.int32, (c_rows, c_rows), 0))
    for g in range(N_SGU_GROUPS):
        wg = jnp.where(causal, wmix_ref[g], 0.0).astype(BF16)
        cols = slice(g * gd, (g + 1) * gd)
        for ch in range(tm // c_rows):
            rows = slice(ch * c_rows, (ch + 1) * c_rows)
            mixed = _dot(wg, vb[rows, cols]) + bmix_ref[:, cols]
            uz_ref[rows, cols] = (u[rows, cols] * mixed).astype(BF16)


def _sgu_prompt(x, batch, mods, n_seq, norm_g, w_in, v_g, wmix, bmix):
    t, d = x.shape
    tm = TOKEN_TILE
    tiles_per_batch = t // batch // tm
    return pl.pallas_call(
        functools.partial(_sgu_prompt_kernel, tiles_per_batch),
        grid=(t // tm,),
        in_specs=[
            pl.BlockSpec((tm, d), lambda i: (i, 0)),
            _mod_spec(mods, 0, n_seq, tiles_per_batch),
            _const_spec(norm_g.shape),
            _const_spec(w_in.shape),
            _const_spec(v_g.shape),
            _const_spec(wmix.shape),
            _const_spec(bmix.shape),
        ],
        out_specs=[pl.BlockSpec((tm, d), lambda i: (i, 0)),
                   pl.BlockSpec((None, CHUNK, d), lambda i: (i // tiles_per_batch, 0, 0))],
        out_shape=[jax.ShapeDtypeStruct((t, d), BF16), jax.ShapeDtypeStruct((batch, CHUNK, d), F32)],
        compiler_params=_params(1),
        name="sgu_prompt",
    )(x, mods, norm_g, w_in, v_g, wmix, bmix)


def _sgu_sample_kernel(nq, x_ref, mod_ref, g_ref, win_ref, vg_ref, wcoef_ref, bcoef_ref, uz_ref, v_ref):
    tm = x_ref.shape[0]
    ns = tm // nq
    u, v = _sgu_front(x_ref, mod_ref[...], g_ref, win_ref, vg_ref)
    v_ref[...] = v
    for t in range(nq):
        mixed = bcoef_ref[t:t + 1, :]
        for s in range(t + 1):
            mixed = mixed + wcoef_ref[t * nq + s:t * nq + s + 1, :] * v[s * ns:(s + 1) * ns, :]
        uz_ref[t * ns:(t + 1) * ns, :] = (u[t * ns:(t + 1) * ns, :] * mixed).astype(BF16)


def _sgu_sample(x, nq, mods, n_seq, norm_g, w_in, v_g, wcoef, bcoef):
    t, d = x.shape
    full = pl.BlockSpec((t, d), lambda i: (0, 0))
    return pl.pallas_call(
        functools.partial(_sgu_sample_kernel, nq),
        grid=(1,),
        in_specs=[
            full,
            _mod_spec(mods, 0, n_seq, 0),
            _const_spec(norm_g.shape),
            _const_spec(w_in.shape),
            _const_spec(v_g.shape),
            _const_spec(wcoef.shape),
            _const_spec(bcoef.shape),
        ],
        out_specs=[full, full],
        out_shape=[jax.ShapeDtypeStruct((t, d), BF16), jax.ShapeDtypeStruct((t, d), F32)],
        compiler_params=_params(1),
        name="sgu_sample",
    )(x, mods, norm_g, w_in, v_g, wcoef, bcoef)


def _ff_chunks(d_ff):
    step = 768
    return tuple((s, min(step, d_ff - s)) for s in range(0, d_ff, step))


def _ffn_kernel(layer, tiles_per_batch, x_ref, z_ref, mod_ref, wp_ref, g_ref, win_ref, wout_ref, o_ref):
    tm, d = x_ref.shape
    d_ff = wout_ref.shape[0]
    mod = _mod_rows(mod_ref, tiles_per_batch)
    gate1 = _per_token(mod[:, 2 * d:3 * d], tm)
    gate2 = _per_token(mod[:, 5 * d:6 * d], tm)
    x1 = x_ref[...] + gate1 * _dot(z_ref[...].astype(BF16), wp_ref[...])
    h = _modulate(x1, g_ref[layer:layer + 1, :], mod[:, 3 * d:4 * d], mod[:, 4 * d:5 * d]).astype(BF16)
    acc = None
    for s, w in _ff_chunks(d_ff):
        a = _dot(h, win_ref[:, s:s + w])
        b = _dot(h, win_ref[:, d_ff + s:d_ff + s + w])
        act = (a * jax.nn.sigmoid(a) * b).astype(BF16)
        part = _dot(act, wout_ref[s:s + w, :])
        acc = part if acc is None else acc + part
    o_ref[...] = x1 + gate2 * acc


def _ffn(x, z, layer, mods, n_seq, tiles_per_batch, w_proj, norm_g, w_in, w_out):
    t, d = x.shape
    tm = TOKEN_TILE
    return pl.pallas_call(
        functools.partial(_ffn_kernel, layer, tiles_per_batch),
        grid=(t // tm,),
        in_specs=[
            pl.BlockSpec((tm, d), lambda i: (i, 0)),
            pl.BlockSpec((tm, d), lambda i: (i, 0)),
            _mod_spec(mods, layer, n_seq, tiles_per_batch),
            _const_spec(w_proj.shape),
            _const_spec(norm_g.shape),
            _layer_spec(w_in, layer),
            _layer_spec(w_out, layer),
        ],
        out_specs=pl.BlockSpec((tm, d), lambda i: (i, 0)),
        out_shape=jax.ShapeDtypeStruct((t, d), F32),
        compiler_params=_params(1),
        name="ffn",
    )(x, z, mods, w_proj, norm_g, w_in, w_out)


def _kvq_prompt_kernel(tiles_per_batch, x_ref, kvmod_ref, mod_ref, kvg_ref, n1g_ref, wkt_ref, wvt_ref, wqt_ref,
                       kg_ref, qg_ref, kt_ref, vt_ref, kb_ref, vtb_ref, qt_ref, kmean_ref):
    tm, d = x_ref.shape
    x = x_ref[...]
    kvmod = _mod_rows(kvmod_ref, tiles_per_batch)
    mod = _mod_rows(mod_ref, tiles_per_batch)
    hkv = _modulate(x, kvg_ref[...], kvmod[:, 0:d], kvmod[:, d:2 * d]).astype(BF16)
    kt = _head_rms_t(_dot_nt(wkt_ref[...], hkv), kg_ref[...])
    vt = _dot_nt(wvt_ref[...], hkv)
    kt_ref[...] = kt
    vt_ref[...] = vt
    vtb_ref[...] = vt.astype(BF16)
    hq = _modulate(x, n1g_ref[1:2, :], mod[:, 0:d], mod[:, d:2 * d]).astype(BF16)
    qt_ref[...] = _head_rms_t(_dot_nt(wqt_ref[...], hq), qg_ref[...]).astype(BF16)
    for i in range(tm // MOBA_BLOCK):
        cols = slice(i * MOBA_BLOCK, (i + 1) * MOBA_BLOCK)
        k_nat = kt[:, cols].T
        kmean_ref[i] = jnp.mean(k_nat, axis=0, keepdims=True)
        kb_ref[cols, :] = k_nat.astype(BF16)


def _kvq_prompt(x, batch, kvmods, mods, n_seq, kv_norm_g, norm1_g, w_kt, w_vt, w_qt, k_gain_col, q_gain_col):
    t, d = x.shape
    tm = TOKEN_TILE
    seq = t // batch
    tiles_per_batch = seq // tm
    nblk = tm // MOBA_BLOCK
    row_spec = pl.BlockSpec((tm, d), lambda i: (i, 0))
    col_spec = pl.BlockSpec((None, d, tm), lambda i: (i // tiles_per_batch, 0, i % tiles_per_batch))
    return pl.pallas_call(
        functools.partial(_kvq_prompt_kernel, tiles_per_batch),
        grid=(t // tm,),
        in_specs=[
            row_spec,
            _mod_spec(kvmods, 0, n_seq, tiles_per_batch),
            _mod_spec(mods, 1, n_seq, tiles_per_batch),
            _const_spec(kv_norm_g.shape), _const_spec(norm1_g.shape),
            _const_spec(w_kt.shape), _const_spec(w_vt.shape), _const_spec(w_qt.shape),
            _const_spec((d, 1)), _const_spec((d, 1)),
        ],
        out_specs=[
            col_spec, col_spec, row_spec, col_spec, col_spec,
            pl.BlockSpec((nblk, 1, d), lambda i: (i, 0, 0)),
        ],
        out_shape=[
            jax.ShapeDtypeStruct((batch, d, seq), F32),
            jax.ShapeDtypeStruct((batch, d, seq), F32),
            jax.ShapeDtypeStruct((t, d), BF16),
            jax.ShapeDtypeStruct((batch, d, seq), BF16),
            jax.ShapeDtypeStruct((batch, d, seq), BF16),
            jax.ShapeDtypeStruct((t // MOBA_BLOCK, 1, d), F32),
        ],
        compiler_params=_params(1),
        name="kvq_prompt",
    )(x, kvmods, mods, kv_norm_g, norm1_g, w_kt, w_vt, w_qt, k_gain_col, q_gain_col)


def _kvq_sample_kernel(nq, x_ref, kvmod_ref, mod_ref, kvg_ref, n1g_ref, wk_ref, wv_ref, wq_ref, kg_ref, qg_ref,
                       pool_ref, expand_ref, k_ref, v_ref, q_ref, kt_ref, vt_ref):
    tm, d = x_ref.shape
    ns = tm // nq
    x = x_ref[...]
    kvmod = kvmod_ref[...]
    mod = mod_ref[...]
    hkv = _modulate(x, kvg_ref[...], kvmod[:, 0:d], kvmod[:, d:2 * d]).astype(BF16)
    k = _head_rms(_dot(hkv, wk_ref[...]), kg_ref[...], pool_ref, expand_ref)
    v = _dot(hkv, wv_ref[...])
    k_ref[...] = k
    v_ref[...] = v
    for t in range(nq):
        kt_ref[t] = k[t * ns:(t + 1) * ns, :].T
        vt_ref[t] = v[t * ns:(t + 1) * ns, :].T
    hq = _modulate(x, n1g_ref[1:2, :], mod[:, 0:d], mod[:, d:2 * d]).astype(BF16)
    q_ref[...] = _head_rms(_dot(hq, wq_ref[...]), qg_ref[...], pool_ref, expand_ref)


def _kvq_sample(x, nq, kvmods, mods, n_seq, kv_norm_g, norm1_g, w_k, w_v, w_q, k_gain, q_gain, pool, expand):
    t, d = x.shape
    full = pl.BlockSpec((t, d), lambda i: (0, 0))
    tspec = pl.BlockSpec((nq, d, t // nq), lambda i: (0, 0, 0))
    return pl.pallas_call(
        functools.partial(_kvq_sample_kernel, nq),
        grid=(1,),
        in_specs=[
            full,
            _mod_spec(kvmods, 0, n_seq, 0),
            _mod_spec(mods, 1, n_seq, 0),
            _const_spec(kv_norm_g.shape), _const_spec(norm1_g.shape),
            _const_spec(w_k.shape), _const_spec(w_v.shape), _const_spec(w_q.shape),
            _const_spec((1, d)), _const_spec((1, d)),
            _const_spec(pool.shape), _const_spec(expand.shape),
        ],
        out_specs=[full, full, full, tspec, tspec],
        out_shape=[jax.ShapeDtypeStruct((t, d), F32)] * 3 + [jax.ShapeDtypeStruct((nq, d, t // nq), F32)] * 2,
        compiler_params=_params(1),
        name="kvq_sample",
    )(x, kvmods, mods, kv_norm_g, norm1_g, w_k, w_v, w_q, k_gain, q_gain, pool, expand)


def _top_k_additive_mask(gate, own, axis):
    nb = gate.shape[axis]
    idx = lax.broadcasted_iota(jnp.int32, gate.shape, axis)
    eligible = idx < own
    g = jnp.where(eligible, gate, NEG_INF)
    mask = jnp.full(gate.shape, NEG_INF, F32)
    for _ in range(min(MOBA_TOP_K, nb)):
        best = jnp.max(g, axis=axis, keepdims=True)
        first = jnp.min(jnp.where(g == best, idx, nb), axis=axis, keepdims=True)
        pick = idx == first
        mask = jnp.where(pick, 0.0, mask)
        g = jnp.where(pick, NEG_INF, g)
    return jnp.where(eligible, mask, NEG_INF)


def _prompt_attention(j, slopes_ref, qt_ref, kb_ref, vt_ref, kmean_ref, o_ref, hooks):
    group = pl.program_id(1)
    blk = MOBA_BLOCK
    pair_w = LANES
    n_heads = 2 * (qt_ref.shape[0] // pair_w)
    dh = pair_w // 2
    zeros = jnp.zeros((dh, blk), BF16)
    causal = (lax.broadcasted_iota(jnp.int32, (blk, blk), 0)
              <= lax.broadcasted_iota(jnp.int32, (blk, blk), 1))
    n_feat = 3
    k_feat = jnp.where(lax.broadcasted_iota(jnp.int32, (blk, LANES), 1) < n_feat,
                       lax.broadcasted_iota(jnp.int32, (blk, LANES), 0), 0).astype(F32).astype(BF16)
    feat_row = lax.broadcasted_iota(jnp.int32, (LANES, blk), 0)
    ones_rows = jnp.ones((16, blk), BF16)

    def prepare(h):
        lanes = slice((h // 2) * pair_w, (h // 2 + 1) * pair_w)
        slope = slopes_ref[group * n_heads + h]
        qt = qt_ref[lanes, :]
        qh = jnp.concatenate([qt[:dh], zeros] if h % 2 == 0 else [zeros, qt[dh:]], axis=0)
        q_feat = jnp.zeros((LANES, blk), F32)
        rest_slope = jnp.full((1, blk), slope, F32)
        for f in range(n_feat):
            piece = rest_slope.astype(BF16).astype(F32)
            q_feat = jnp.where(feat_row == f, piece, q_feat)
            rest_slope = rest_slope - piece
        q_aug = jnp.concatenate([qh, q_feat.astype(BF16)], axis=0)
        rows = []
        if j > 0:
            km_hi, km_lo = _split_bf16(kmean_ref[:, lanes])
            sel = _top_k_additive_mask(_dot(km_hi, qh) + _dot(km_lo, qh), j, 0)
            rows = [sel[i:i + 1, :] + slope * float((i - j) * blk) for i in range(j)]
        rows.append(jnp.zeros((1, blk), F32))
        return q_aug, rows

    def scores(h, q_aug, i):
        lanes = slice((h // 2) * pair_w, (h // 2 + 1) * pair_w)
        t = _dot(jnp.concatenate([kb_ref[i * blk:(i + 1) * blk, lanes], k_feat], axis=1), q_aug)
        return jnp.where(causal, t, NEG_INF) if i == j else t

    def values(h, t, shift, i):
        p = jnp.exp2((t - shift).astype(BF16))
        vt_aug = jnp.concatenate([vt_ref[h * dh:(h + 1) * dh, i * blk:(i + 1) * blk], ones_rows], axis=0)
        return _dot(vt_aug, p)

    outs = []
    prev = None
    for h in range(n_heads + 1):
        if h in hooks:
            hooks[h]()
        if h < n_heads:
            q_aug, rows = prepare(h)
            ts, m = [], None
        acc = None
        for i in range(j + 1):
            if h < n_heads:
                t = scores(h, q_aug, i)
                ts.append(t)
                mi = jnp.max(t, axis=0, keepdims=True) + rows[i]
                m = mi if m is None else jnp.maximum(m, mi)
            if prev is not None:
                p_ts, p_m, p_rows = prev
                part = values(h - 1, p_ts[i], p_m - p_rows[i], i)
                acc = part if acc is None else acc + part
        if prev is not None:
            outs.append(acc[:dh] / acc[dh:dh + 1])
        prev = (ts, m, rows) if h < n_heads else None
    o_ref[...] = jnp.concatenate(outs, axis=0).T.astype(o_ref.dtype)


def _sample_attention(n_pages, r, q_ref, kn_ref, vn_ref, slope_ref, kbuf, vbuf, o_ref):
    nq, _, _, d = q_ref.shape
    dh = d // N_HEADS
    rows = nq * N_HEADS
    pages_per_block = MOBA_BLOCK // PAGE_SIZE
    n_past = n_pages // pages_per_block
    past_len = n_pages * PAGE_SIZE

    lane_head = lax.broadcasted_iota(jnp.int32, (N_HEADS, d), 1) // dh
    head_mask = lane_head == lax.broadcasted_iota(jnp.int32, (N_HEADS, d), 0)
    qrows = jnp.concatenate([jnp.where(head_mask, q_ref[i, r], 0.0) for i in range(nq)], axis=0)
    qrows_b = qrows.astype(BF16)
    row_head_mask = jnp.concatenate([head_mask] * nq, axis=0)
    slope = slope_ref[...]
    row_q = lax.broadcasted_iota(jnp.int32, (rows, 1), 0) // N_HEADS

    scores = [_dot(qrows_b, kbuf[p].astype(BF16)) for p in range(n_pages)]
    gate = jnp.concatenate(
        [sum(jnp.sum(s, axis=1, keepdims=True) for s in scores[n * pages_per_block:(n + 1) * pages_per_block])
         for n in range(n_past)], axis=1) * (1.0 / MOBA_BLOCK)
    sel = _top_k_additive_mask(gate, n_past, 1)

    dist0 = (past_len + row_q).astype(F32)
    offs = lax.broadcasted_iota(jnp.int32, (1, PAGE_SIZE), 1)
    masked = []
    for p in range(n_pages):
        n = p // pages_per_block
        kpos = (p * PAGE_SIZE + offs).astype(F32)
        masked.append(scores[p] - slope * (dist0 - kpos) + sel[:, n:n + 1])
    own = []
    for t in range(nq):
        s = jnp.sum(qrows * kn_ref[t, r], axis=1, keepdims=True)
        s = s - slope * (row_q - t).astype(F32)
        own.append(jnp.where(row_q >= t, s, NEG_INF))
    m = own[0]
    for s in own[1:]:
        m = jnp.maximum(m, s)
    for s in masked:
        m = jnp.maximum(m, jnp.max(s, axis=1, keepdims=True))
    l = jnp.zeros((rows, 1), F32)
    acc = jnp.zeros((rows, d), F32)
    for t in range(nq):
        p_own = jnp.exp(own[t] - m)
        l = l + p_own
        acc = acc + p_own * vn_ref[t, r]
    for p in range(n_pages):
        pr = jnp.exp(masked[p] - m)
        l = l + jnp.sum(pr, axis=1, keepdims=True)
        acc = acc + _dot_nt(pr.astype(BF16), vbuf[p].astype(BF16))
    o_rows = jnp.where(row_head_mask, acc / l, 0.0)
    o = jnp.sum(o_rows.reshape(nq, N_HEADS, d), axis=1)
    for t in range(nq):
        o_ref[t, r] = o[t:t + 1, :]


def _attn_kernel(j, n_slots, seq_base, n_pages, pt_ref, slopes_ref, qt_ref, kb_ref, vt_ref, kmean_ref,
                 o_in_ref, *rest):
    if not n_slots:
        (o_ref,) = rest
        _prompt_attention(j, slopes_ref, qt_ref, kb_ref, vt_ref, kmean_ref, o_ref, {})
        return
    qs_ref, kn_ref, vn_ref, srow_ref, ck_ref, cv_ref, os_in_ref, o_ref, os_ref, kbuf, vbuf, sem = rest
    step = pl.program_id(0) * pl.num_programs(1) + pl.program_id(1)

    def page_copies(r):
        seq = seq_base + step * n_slots + r
        copies = []
        for i in range(n_pages):
            page = pt_ref[seq * n_pages + i]
            copies.append(pltpu.make_async_copy(ck_ref.at[page], kbuf.at[i], sem.at[0, i]))
            copies.append(pltpu.make_async_copy(cv_ref.at[page], vbuf.at[i], sem.at[1, i]))
        return copies

    n_seq = os_in_ref.shape[1]

    def start(r):
        @pl.when(seq_base + step * n_slots + r < n_seq)
        def _():
            for c in page_copies(r):
                c.start()

    def finish(r):
        @pl.when(seq_base + step * n_slots + r < n_seq)
        def _():
            for c in page_copies(r):
                c.wait()
            _sample_attention(n_pages, r, qs_ref, kn_ref, vn_ref, srow_ref, kbuf, vbuf, os_ref)

    n_stages = 2 * (qt_ref.shape[0] // LANES) + 1
    hooks = {0: functools.partial(start, 0)}
    for r in range(1, n_slots):
        def swap(r=r):
            finish(r - 1)
            start(r)
        hooks[(r * n_stages) // n_slots] = swap
    _prompt_attention(j, slopes_ref, qt_ref, kb_ref, vt_ref, kmean_ref, o_ref, hooks)
    finish(n_slots - 1)


def _sample_slots(n_blocks, steps, n_seq, max_slots=2):
    assert n_seq % steps == 0
    left = n_seq // steps
    slots = [0] * n_blocks
    for j in reversed(range(n_blocks)):
        slots[j] = min(max_slots, left)
        left -= slots[j]
    assert left == 0
    return slots


def _attention(qt, kb, vt, kmean, slopes, page_table, q_s, k_s, v_s, slope_rows, cache_k, cache_v):
    batch, d, seq = qt.shape
    nq, n_seq, _ = q_s.shape
    blk = MOBA_BLOCK
    n_blocks = seq // blk
    width = ATTN_HEADS_PER_STEP * (d // N_HEADS)
    groups = d // width
    n_pages = page_table.shape[1]
    pool_pages = cache_k.shape[0]
    ck = cache_k.transpose(0, 2, 3, 1).reshape(pool_pages, d, PAGE_SIZE)
    cv = cache_v.transpose(0, 2, 3, 1).reshape(pool_pages, d, PAGE_SIZE)
    per_seq = lambda a: a.reshape(nq, n_seq, 1, d)
    slots = _sample_slots(n_blocks, batch * groups, n_seq)
    pt = page_table.reshape(-1)
    out = jnp.zeros((batch, seq, d), BF16)
    out_s = jnp.zeros((nq, n_seq, 1, d), F32)
    seq_base = 0
    for j in range(n_blocks):
        keys = (j + 1) * blk
        n_slots = slots[j]
        in_specs = [
            pl.BlockSpec(memory_space=pltpu.SMEM),
            pl.BlockSpec((None, width, blk), lambda b, g, pt, j=j: (b, g, j)),
            pl.BlockSpec((None, keys, width), lambda b, g, pt: (b, 0, g)),
            pl.BlockSpec((None, width, keys), lambda b, g, pt: (b, g, 0)),
            pl.BlockSpec((None, n_blocks, width), lambda b, g, pt: (b, 0, g)),
            pl.BlockSpec(memory_space=pl.ANY),
        ]
        args = [pt, slopes, qt, kb, vt, kmean, out]
        out_specs = [pl.BlockSpec((None, blk, width), lambda b, g, pt, j=j: (b, j, g))]
        out_shape = [jax.ShapeDtypeStruct((batch, seq, d), BF16)]
        scratch = []
        aliases = {len(args) - 1: 0}
        if n_slots:
            first_block = seq_base // n_slots
            assert seq_base % n_slots == 0
            slot_spec = pl.BlockSpec((nq, n_slots, 1, d),
                                     lambda b, g, pt, f=first_block: (0, f + b * groups + g, 0, 0))
            in_specs += [slot_spec, slot_spec, slot_spec,
                         pl.BlockSpec(slope_rows.shape, lambda b, g, pt: (0, 0)),
                         pl.BlockSpec(memory_space=pl.ANY), pl.BlockSpec(memory_space=pl.ANY),
                         pl.BlockSpec(memory_space=pl.ANY)]
            args += [per_seq(q_s), per_seq(k_s), per_seq(v_s), slope_rows, ck, cv, out_s]
            aliases[len(args) - 1] = 1
            out_specs.append(slot_spec)
            out_shape.append(jax.ShapeDtypeStruct((nq, n_seq, 1, d), F32))
            scratch = [pltpu.VMEM((n_pages, d, PAGE_SIZE), F32), pltpu.VMEM((n_pages, d, PAGE_SIZE), F32),
                       pltpu.SemaphoreType.DMA((2, n_pages))]
        res = pl.pallas_call(
            functools.partial(_attn_kernel, j, n_slots, seq_base, n_pages),
            grid_spec=pltpu.PrefetchScalarGridSpec(
                num_scalar_prefetch=1, grid=(batch, groups), in_specs=in_specs, out_specs=out_specs,
                scratch_shapes=scratch),
            out_shape=out_shape,
            input_output_aliases=aliases,
            compiler_params=_params(2),
            name=f"attn_{j}",
        )(*args)
        out = res[0]
        if n_slots:
            out_s = res[1]
            seq_base += n_slots * batch * groups
    return out, out_s


def kernel(x_prompt, x_sample, c_prompt, c_sample, cache_k, cache_v, page_table, ada_w, ada_b, norm1_g, norm2_g,
           ffn_w_in, ffn_w_out, sgu_w_in, sgu_v_g, sgu_w_s, sgu_b_s, sgu_w_out, kv_ada_w, kv_ada_b, kv_norm_g,
           w_kv, k_norm_g, attn_w_q, q_norm_g, attn_w_o):
    batch, seq, d = x_prompt.shape
    n_seq, nq, _ = x_sample.shape
    dh = d // N_HEADS
    gd = d // N_SGU_GROUPS
    n_tok_p = batch * seq
    n_tok_s = n_seq * nq
    assert seq % TOKEN_TILE == 0 and n_tok_s == TOKEN_TILE and n_seq % SUBLANES == 0 and batch <= SUBLANES

    c_all = jnp.concatenate([c_sample, c_prompt, jnp.zeros((SUBLANES - batch, d), F32)], axis=0)
    mods = _ada(c_all, ada_w, ada_b, 1536)
    kvmods = _ada(c_all, kv_ada_w[None], kv_ada_b[None], 1024)

    bf = lambda w: w.astype(BF16)
    sgu_w_in_b, sgu_w_out_b = bf(sgu_w_in[0]), bf(sgu_w_out[0])
    ffn_in_b, ffn_out_b = bf(ffn_w_in), bf(ffn_w_out)
    w_k_b, w_v_b = bf(w_kv[:, :d]), bf(w_kv[:, d:])
    w_q_b, w_qt_b, w_o_b = bf(attn_w_q[0]), bf(attn_w_q[0].T), bf(attn_w_o[0])
    k_gain = jnp.tile(k_norm_g, N_HEADS).reshape(1, d)
    q_gain = (jnp.tile(q_norm_g[0], N_HEADS) * (dh ** -0.5)).reshape(1, d)
    kv_norm_row = kv_norm_g.reshape(1, d)
    head_of_lane = np.arange(d) // dh
    pool = jnp.asarray((head_of_lane[:, None] == np.arange(LANES)[None, :]) / dh, BF16)
    expand = jnp.asarray(np.arange(LANES)[:, None] == head_of_lane[None, :], BF16)
    slopes = jnp.exp2(-8.0 * jnp.arange(1, N_HEADS + 1, dtype=F32) / N_HEADS)

    wmix_p = sgu_w_s[0]
    bmix_p = jnp.repeat(sgu_b_s[0].T, gd, axis=1)
    wcoef_s = jnp.repeat(sgu_w_s[0][:, :nq, :nq].transpose(1, 2, 0).reshape(nq * nq, N_SGU_GROUPS), gd, axis=1)
    bcoef_s = jnp.repeat(sgu_b_s[0][:, :nq].T, gd, axis=1)

    xp = x_prompt.reshape(n_tok_p, d)
    xs = x_sample.transpose(1, 0, 2).reshape(n_tok_s, d)
    tiles_pb = seq // TOKEN_TILE

    uz_p, sguv_p = _sgu_prompt(xp, batch, mods, n_seq, norm1_g, sgu_w_in_b, sgu_v_g, wmix_p, bmix_p)
    x2_p = _ffn(xp, uz_p, 0, mods, n_seq, tiles_pb, sgu_w_out_b, norm2_g, ffn_in_b, ffn_out_b)
    kt_p, vt_p, kb_p, vtb_p, qt_p, kmean_p = _kvq_prompt(
        x2_p, batch, kvmods, mods, n_seq, kv_norm_row, norm1_g, bf(w_kv[:, :d].T), bf(w_kv[:, d:].T), w_qt_b,
        k_gain.reshape(d, 1), (q_gain * LOG2_E).reshape(d, 1))
    uz_s, sguv_s = _sgu_sample(xs, nq, mods, n_seq, norm1_g, sgu_w_in_b, sgu_v_g, wcoef_s, bcoef_s)
    x2_s = _ffn(xs, uz_s, 0, mods, n_seq, 0, sgu_w_out_b, norm2_g, ffn_in_b, ffn_out_b)
    k_s, v_s, q_s, kt_s, vt_s = _kvq_sample(x2_s, nq, kvmods, mods, n_seq, kv_norm_row, norm1_g, w_k_b, w_v_b,
                                            w_q_b, k_gain, q_gain, pool, expand)

    slope_rows = jnp.tile(slopes, nq).reshape(nq * N_HEADS, 1)
    o_p, o_s = _attention(qt_p, kb_p.reshape(batch, seq, d), vtb_p, kmean_p.reshape(batch, seq // MOBA_BLOCK, d),
                          slopes * LOG2_E, page_table, q_s.reshape(nq, n_seq, d), k_s.reshape(nq, n_seq, d),
                          v_s.reshape(nq, n_seq, d), slope_rows, cache_k, cache_v)
    y_p = _ffn(x2_p, o_p.reshape(n_tok_p, d), 1, mods, n_seq, tiles_pb, w_o_b, norm2_g, ffn_in_b, ffn_out_b)
    y_s = _ffn(x2_s, o_s.reshape(n_tok_s, d), 1, mods, n_seq, 0, w_o_b, norm2_g, ffn_in_b, ffn_out_b)

    by_seq = lambda a: a.reshape(nq, n_seq, d).transpose(1, 0, 2)
    return (y_p.reshape(batch, seq, d), by_seq(y_s),
            kt_p.reshape(batch, N_HEADS, dh, seq).transpose(0, 3, 1, 2),
            vt_p.reshape(batch, N_HEADS, dh, seq).transpose(0, 3, 1, 2),
            kt_s.reshape(nq, N_HEADS, dh, n_seq).transpose(3, 0, 1, 2),
            vt_s.reshape(nq, N_HEADS, dh, n_seq).transpose(3, 0, 1, 2),
            sguv_p.reshape(1, batch, CHUNK, d), by_seq(sguv_s)[None])
```

```python
import functools

import numpy as np
import jax
import jax.numpy as jnp
from jax import lax
from jax.experimental import pallas as pl
from jax.experimental.pallas import tpu as pltpu

F32 = jnp.float32
BF16 = jnp.bfloat16
EPS = 1e-6
NEG_INF = float("-inf")
LOG2_E = 1.4426950408889634

V7X_VMEM_LIMIT_BYTES = 56 * 1024 * 1024
LANES = 128
SUBLANES = 8

N_HEADS = 16
MOBA_BLOCK = 256
MOBA_TOP_K = 3
CHUNK = 128
N_SGU_GROUPS = 8
PAGE_SIZE = 128
TOKEN_TILE = 512
ATTN_HEADS_PER_STEP = 8
ATTN_HEADS_PER_FUSED_STEP = 4


def _params(n_grid_dims):
    return pltpu.CompilerParams(
        dimension_semantics=("arbitrary",) * n_grid_dims,
        vmem_limit_bytes=V7X_VMEM_LIMIT_BYTES)


def _const_spec(shape):
    n = len(shape)
    return pl.BlockSpec(shape, lambda *_: (0,) * n, pipeline_mode=pl.Buffered(1))


def _layer_spec(w, layer):
    tail = (0,) * (w.ndim - 1)
    return pl.BlockSpec((None,) + w.shape[1:], lambda *_: (layer,) + tail, pipeline_mode=pl.Buffered(1))


def _mod_spec(mods, layer, n_seq, tiles_per_batch):
    w = mods.shape[-1]
    if tiles_per_batch:
        return pl.BlockSpec((None, SUBLANES, w), lambda *_: (layer, n_seq // SUBLANES, 0))
    return pl.BlockSpec((None, n_seq, w), lambda *_: (layer, 0, 0))


def _mod_rows(mod_ref, tiles_per_batch):
    if tiles_per_batch:
        return mod_ref[pl.ds(pl.program_id(0) // tiles_per_batch, 1), :]
    return mod_ref[...]


def _per_token(v, tm):
    if v.shape[0] in (1, tm):
        return v
    return jnp.concatenate([v] * (tm // v.shape[0]), axis=0)


def _dot(a, b):
    return jnp.dot(a, b, preferred_element_type=F32)


def _dot_nt(a, b):
    return lax.dot_general(a, b, (((1,), (1,)), ((), ())), preferred_element_type=F32)


def _split_bf16(x):
    hi = x.astype(BF16)
    lo = (x - hi.astype(F32)).astype(BF16)
    return hi, lo


def _rms(x, g):
    return x * lax.rsqrt(jnp.mean(x * x, axis=-1, keepdims=True) + EPS) * g


def _modulate(x, g, shift, scale):
    tm = x.shape[0]
    return _rms(x, g) * (1.0 + _per_token(scale, tm)) + _per_token(shift, tm)


def _head_rms(x, gain, pool_ref, expand_ref):
    sq_hi, sq_lo = _split_bf16(x * x)
    pool = pool_ref[...]
    ms = _dot(sq_hi, pool) + _dot(sq_lo, pool)
    r_hi, r_lo = _split_bf16(lax.rsqrt(ms + EPS))
    expand = expand_ref[...]
    return x * (_dot(r_hi, expand) + _dot(r_lo, expand)) * gain


def _head_rms_t(xt, gain_col):
    d, tm = xt.shape
    x3 = xt.reshape(N_HEADS, d // N_HEADS, tm)
    x3 = x3 * lax.rsqrt(jnp.mean(x3 * x3, axis=1, keepdims=True) + EPS)
    return x3.reshape(d, tm) * gain_col


def _ada_kernel(c_ref, w_ref, b_ref, o_ref):
    c = c_ref[...]
    s = (c * jax.nn.sigmoid(c)).astype(BF16)
    o_ref[...] = _dot(s, w_ref[...].astype(BF16)) + b_ref[pl.ds(pl.program_id(0), 1), :]


def _ada(c_all, w, b, tn):
    n_layers, d, n = w.shape
    m = c_all.shape[0]
    return pl.pallas_call(
        _ada_kernel,
        grid=(n_layers, n // tn),
        in_specs=[
            pl.BlockSpec((m, d), lambda l, j: (0, 0)),
            pl.BlockSpec((None, d, tn), lambda l, j: (l, 0, j)),
            pl.BlockSpec((n_layers, tn), lambda l, j: (0, j)),
        ],
        out_specs=pl.BlockSpec((None, m, tn), lambda l, j: (l, 0, j)),
        out_shape=jax.ShapeDtypeStruct((n_layers, m, n), F32),
        compiler_params=_params(2),
        name="ada",
    )(c_all, w, b)


def _sgu_front(x_ref, mod, g_ref, win_ref, vg_ref):
    d = x_ref.shape[1]
    h = _modulate(x_ref[...], g_ref[0:1, :], mod[:, 0:d], mod[:, d:2 * d]).astype(BF16)
    uv = jax.nn.gelu(_dot(h, win_ref[...]))
    return uv[:, :d], _rms(uv[:, d:], vg_ref[...])


def _sgu_prompt_kernel(tiles_per_batch, x_ref, mod_ref, g_ref, win_ref, vg_ref, wmix_ref, bmix_ref, uz_ref, v_ref):
    tm, d = x_ref.shape
    c_rows = wmix_ref.shape[1]
    gd = d // N_SGU_GROUPS
    u, v = _sgu_front(x_ref, _mod_rows(mod_ref, tiles_per_batch), g_ref, win_ref, vg_ref)
    v_ref[...] = v[tm - v_ref.shape[0]:, :]
    vb = v.astype(BF16)
    causal = (lax.broadcasted_iota(jnp.int32, (c_rows, c_rows), 1)
              <= lax.broadcasted_iota(jnp.int32, (c_rows, c_rows), 0))
    for g in range(N_SGU_GROUPS):
        wg = jnp.where(causal, wmix_ref[g], 0.0).astype(BF16)
        cols = slice(g * gd, (g + 1) * gd)
        for ch in range(tm // c_rows):
            rows = slice(ch * c_rows, (ch + 1) * c_rows)
            mixed = _dot(wg, vb[rows, cols]) + bmix_ref[:, cols]
            uz_ref[rows, cols] = (u[rows, cols] * mixed).astype(BF16)


def _sgu_prompt(x, batch, mods, n_seq, norm_g, w_in, v_g, wmix, bmix):
    t, d = x.shape
    tm = TOKEN_TILE
    tiles_per_batch = t // batch // tm
    return pl.pallas_call(
        functools.partial(_sgu_prompt_kernel, tiles_per_batch),
        grid=(t // tm,),
        in_specs=[
            pl.BlockSpec((tm, d), lambda i: (i, 0)),
            _mod_spec(mods, 0, n_seq, tiles_per_batch),
            _const_spec(norm_g.shape),
            _const_spec(w_in.shape),
            _const_spec(v_g.shape),
            _const_spec(wmix.shape),
            _const_spec(bmix.shape),
        ],
        out_specs=[pl.BlockSpec((tm, d), lambda i: (i, 0)),
                   pl.BlockSpec((None, CHUNK, d), lambda i: (i // tiles_per_batch, 0, 0))],
        out_shape=[jax.ShapeDtypeStruct((t, d), BF16), jax.ShapeDtypeStruct((batch, CHUNK, d), F32)],
        compiler_params=_params(1),
        name="sgu_prompt",
    )(x, mods, norm_g, w_in, v_g, wmix, bmix)


def _sgu_sample_kernel(nq, x_ref, mod_ref, g_ref, win_ref, vg_ref, wcoef_ref, bcoef_ref, uz_ref, v_ref):
    tm = x_ref.shape[0]
    ns = tm // nq
    u, v = _sgu_front(x_ref, mod_ref[...], g_ref, win_ref, vg_ref)
    v_ref[...] = v
    for t in range(nq):
        mixed = bcoef_ref[t:t + 1, :]
        for s in range(t + 1):
            mixed = mixed + wcoef_ref[t * nq + s:t * nq + s + 1, :] * v[s * ns:(s + 1) * ns, :]
        uz_ref[t * ns:(t + 1) * ns, :] = (u[t * ns:(t + 1) * ns, :] * mixed).astype(BF16)


def _sgu_sample(x, nq, mods, n_seq, norm_g, w_in, v_g, wcoef, bcoef):
    t, d = x.shape
    full = pl.BlockSpec((t, d), lambda i: (0, 0))
    return pl.pallas_call(
        functools.partial(_sgu_sample_kernel, nq),
        grid=(1,),
        in_specs=[
            full,
            _mod_spec(mods, 0, n_seq, 0),
            _const_spec(norm_g.shape),
            _const_spec(w_in.shape),
            _const_spec(v_g.shape),
            _const_spec(wcoef.shape),
            _const_spec(bcoef.shape),
        ],
        out_specs=[full, full],
        out_shape=[jax.ShapeDtypeStruct((t, d), BF16), jax.ShapeDtypeStruct((t, d), F32)],
        compiler_params=_params(1),
        name="sgu_sample",
    )(x, mods, norm_g, w_in, v_g, wcoef, bcoef)


def _ff_chunks(d_ff):
    step = 768
    return tuple((s, min(step, d_ff - s)) for s in range(0, d_ff, step))


def _ffn_kernel(layer, tiles_per_batch, x_ref, z_ref, mod_ref, wp_ref, g_ref, win_ref, wout_ref, o_ref):
    tm, d = x_ref.shape
    d_ff = wout_ref.shape[0]
    mod = _mod_rows(mod_ref, tiles_per_batch)
    gate1 = _per_token(mod[:, 2 * d:3 * d], tm)
    gate2 = _per_token(mod[:, 5 * d:6 * d], tm)
    x1 = x_ref[...] + gate1 * _dot(z_ref[...].astype(BF16), wp_ref[...])
    h = _modulate(x1, g_ref[layer:layer + 1, :], mod[:, 3 * d:4 * d], mod[:, 4 * d:5 * d]).astype(BF16)
    acc = None
    for s, w in _ff_chunks(d_ff):
        a = _dot(h, win_ref[:, s:s + w])
        b = _dot(h, win_ref[:, d_ff + s:d_ff + s + w])
        act = (a * jax.nn.sigmoid(a) * b).astype(BF16)
        part = _dot(act, wout_ref[s:s + w, :])
        acc = part if acc is None else acc + part
    o_ref[...] = x1 + gate2 * acc


def _ffn(x, z, layer, mods, n_seq, tiles_per_batch, w_proj, norm_g, w_in, w_out):
    t, d = x.shape
    tm = TOKEN_TILE
    return pl.pallas_call(
        functools.partial(_ffn_kernel, layer, tiles_per_batch),
        grid=(t // tm,),
        in_specs=[
            pl.BlockSpec((tm, d), lambda i: (i, 0)),
            pl.BlockSpec((tm, d), lambda i: (i, 0)),
            _mod_spec(mods, layer, n_seq, tiles_per_batch),
            _const_spec(w_proj.shape),
            _const_spec(norm_g.shape),
            _layer_spec(w_in, layer),
            _layer_spec(w_out, layer),
        ],
        out_specs=pl.BlockSpec((tm, d), lambda i: (i, 0)),
        out_shape=jax.ShapeDtypeStruct((t, d), F32),
        compiler_params=_params(1),
        name="ffn",
    )(x, z, mods, w_proj, norm_g, w_in, w_out)


def _kvq_prompt_kernel(tiles_per_batch, x_ref, kvmod_ref, mod_ref, kvg_ref, n1g_ref, wkt_ref, wvt_ref, wqt_ref,
                       kg_ref, qg_ref, kt_ref, vt_ref, kb_ref, vtb_ref, qt_ref, kmean_ref):
    tm, d = x_ref.shape
    x = x_ref[...]
    kvmod = _mod_rows(kvmod_ref, tiles_per_batch)
    mod = _mod_rows(mod_ref, tiles_per_batch)
    hkv = _modulate(x, kvg_ref[...], kvmod[:, 0:d], kvmod[:, d:2 * d]).astype(BF16)
    kt = _head_rms_t(_dot_nt(wkt_ref[...], hkv), kg_ref[...])
    vt = _dot_nt(wvt_ref[...], hkv)
    kt_ref[...] = kt
    vt_ref[...] = vt
    vtb_ref[...] = vt.astype(BF16)
    hq = _modulate(x, n1g_ref[1:2, :], mod[:, 0:d], mod[:, d:2 * d]).astype(BF16)
    qt_ref[...] = _head_rms_t(_dot_nt(wqt_ref[...], hq), qg_ref[...]).astype(BF16)
    for i in range(tm // MOBA_BLOCK):
        cols = slice(i * MOBA_BLOCK, (i + 1) * MOBA_BLOCK)
        k_nat = kt[:, cols].T
        kmean_ref[i] = jnp.mean(k_nat, axis=0, keepdims=True)
        kb_ref[cols, :] = k_nat.astype(BF16)


def _kvq_prompt(x, batch, kvmods, mods, n_seq, kv_norm_g, norm1_g, w_kt, w_vt, w_qt, k_gain_col, q_gain_col):
    t, d = x.shape
    tm = TOKEN_TILE
    seq = t // batch
    tiles_per_batch = seq // tm
    nblk = tm // MOBA_BLOCK
    row_spec = pl.BlockSpec((tm, d), lambda i: (i, 0))
    col_spec = pl.BlockSpec((None, d, tm), lambda i: (i // tiles_per_batch, 0, i % tiles_per_batch))
    return pl.pallas_call(
        functools.partial(_kvq_prompt_kernel, tiles_per_batch),
        grid=(t // tm,),
        in_specs=[
            row_spec,
            _mod_spec(kvmods, 0, n_seq, tiles_per_batch),
            _mod_spec(mods, 1, n_seq, tiles_per_batch),
            _const_spec(kv_norm_g.shape), _const_spec(norm1_g.shape),
            _const_spec(w_kt.shape), _const_spec(w_vt.shape), _const_spec(w_qt.shape),
            _const_spec((d, 1)), _const_spec((d, 1)),
        ],
        out_specs=[
            col_spec, col_spec, row_spec, col_spec, col_spec,
            pl.BlockSpec((nblk, 1, d), lambda i: (i, 0, 0)),
        ],
        out_shape=[
            jax.ShapeDtypeStruct((batch, d, seq), F32),
            jax.ShapeDtypeStruct((batch, d, seq), F32),
            jax.ShapeDtypeStruct((t, d), BF16),
            jax.ShapeDtypeStruct((batch, d, seq), BF16),
            jax.ShapeDtypeStruct((batch, d, seq), BF16),
            jax.ShapeDtypeStruct((t // MOBA_BLOCK, 1, d), F32),
        ],
        compiler_params=_params(1),
        name="kvq_prompt",
    )(x, kvmods, mods, kv_norm_g, norm1_g, w_kt, w_vt, w_qt, k_gain_col, q_gain_col)


def _kvq_sample_kernel(nq, x_ref, kvmod_ref, mod_ref, kvg_ref, n1g_ref, wk_ref, wv_ref, wq_ref, kg_ref, qg_ref,
                       pool_ref, expand_ref, k_ref, v_ref, q_ref, kt_ref, vt_ref):
    tm, d = x_ref.shape
    ns = tm // nq
    x = x_ref[...]
    kvmod = kvmod_ref[...]
    mod = mod_ref[...]
    hkv = _modulate(x, kvg_ref[...], kvmod[:, 0:d], kvmod[:, d:2 * d]).astype(BF16)
    k = _head_rms(_dot(hkv, wk_ref[...]), kg_ref[...], pool_ref, expand_ref)
    v = _dot(hkv, wv_ref[...])
    k_ref[...] = k
    v_ref[...] = v
    for t in range(nq):
        kt_ref[t] = k[t * ns:(t + 1) * ns, :].T
        vt_ref[t] = v[t * ns:(t + 1) * ns, :].T
    hq = _modulate(x, n1g_ref[1:2, :], mod[:, 0:d], mod[:, d:2 * d]).astype(BF16)
    q_ref[...] = _head_rms(_dot(hq, wq_ref[...]), qg_ref[...], pool_ref, expand_ref)


def _kvq_sample(x, nq, kvmods, mods, n_seq, kv_norm_g, norm1_g, w_k, w_v, w_q, k_gain, q_gain, pool, expand):
    t, d = x.shape
    full = pl.BlockSpec((t, d), lambda i: (0, 0))
    tspec = pl.BlockSpec((nq, d, t // nq), lambda i: (0, 0, 0))
    return pl.pallas_call(
        functools.partial(_kvq_sample_kernel, nq),
        grid=(1,),
        in_specs=[
            full,
            _mod_spec(kvmods, 0, n_seq, 0),
            _mod_spec(mods, 1, n_seq, 0),
            _const_spec(kv_norm_g.shape), _const_spec(norm1_g.shape),
            _const_spec(w_k.shape), _const_spec(w_v.shape), _const_spec(w_q.shape),
            _const_spec((1, d)), _const_spec((1, d)),
            _const_spec(pool.shape), _const_spec(expand.shape),
        ],
        out_specs=[full, full, full, tspec, tspec],
        out_shape=[jax.ShapeDtypeStruct((t, d), F32)] * 3 + [jax.ShapeDtypeStruct((nq, d, t // nq), F32)] * 2,
        compiler_params=_params(1),
        name="kvq_sample",
    )(x, kvmods, mods, kv_norm_g, norm1_g, w_k, w_v, w_q, k_gain, q_gain, pool, expand)


def _top_k_additive_mask(gate, own, axis):
    nb = gate.shape[axis]
    idx = lax.broadcasted_iota(jnp.int32, gate.shape, axis)
    eligible = idx < own
    g = jnp.where(eligible, gate, NEG_INF)
    mask = jnp.full(gate.shape, NEG_INF, F32)
    for _ in range(min(MOBA_TOP_K, nb)):
        best = jnp.max(g, axis=axis, keepdims=True)
        first = jnp.min(jnp.where(g == best, idx, nb), axis=axis, keepdims=True)
        pick = idx == first
        mask = jnp.where(pick, 0.0, mask)
        g = jnp.where(pick, NEG_INF, g)
    return jnp.where(eligible, mask, NEG_INF)


def _prompt_attention(j, slopes_ref, qt_ref, kb_ref, vt_ref, kmean_ref, o_ref, hooks):
    group = pl.program_id(1)
    blk = MOBA_BLOCK
    pair_w = LANES
    n_heads = 2 * (qt_ref.shape[0] // pair_w)
    dh = pair_w // 2
    zeros = jnp.zeros((dh, blk), BF16)
    causal = (lax.broadcasted_iota(jnp.int32, (blk, blk), 0)
              <= lax.broadcasted_iota(jnp.int32, (blk, blk), 1))
    n_feat = 3
    k_feat = jnp.where(lax.broadcasted_iota(jnp.int32, (blk, LANES), 1) < n_feat,
                       lax.broadcasted_iota(jnp.int32, (blk, LANES), 0), 0).astype(F32).astype(BF16)
    feat_row = lax.broadcasted_iota(jnp.int32, (LANES, blk), 0)
    ones_rows = jnp.ones((16, blk), BF16)

    def prepare(h):
        lanes = slice((h // 2) * pair_w, (h // 2 + 1) * pair_w)
        slope = slopes_ref[group * n_heads + h]
        qt = qt_ref[lanes, :]
        qh = jnp.concatenate([qt[:dh], zeros] if h % 2 == 0 else [zeros, qt[dh:]], axis=0)
        q_feat = jnp.zeros((LANES, blk), F32)
        rest_slope = jnp.full((1, blk), slope, F32)
        for f in range(n_feat):
            piece = rest_slope.astype(BF16).astype(F32)
            q_feat = jnp.where(feat_row == f, piece, q_feat)
            rest_slope = rest_slope - piece
        q_aug = jnp.concatenate([qh, q_feat.astype(BF16)], axis=0)
        rows = []
        if j > 0:
            km_hi, km_lo = _split_bf16(kmean_ref[:, lanes])
            sel = _top_k_additive_mask(_dot(km_hi, qh) + _dot(km_lo, qh), j, 0)
            rows = [sel[i:i + 1, :] + slope * float((i - j) * blk) for i in range(j)]
        rows.append(jnp.zeros((1, blk), F32))
        return q_aug, rows

    def scores(h, q_aug, i):
        lanes = slice((h // 2) * pair_w, (h // 2 + 1) * pair_w)
        t = _dot(jnp.concatenate([kb_ref[i * blk:(i + 1) * blk, lanes], k_feat], axis=1), q_aug)
        return jnp.where(causal, t, NEG_INF) if i == j else t

    def values(h, t, shift, i):
        p = jnp.exp2((t - shift).astype(BF16))
        vt_aug = jnp.concatenate([vt_ref[h * dh:(h + 1) * dh, i * blk:(i + 1) * blk], ones_rows], axis=0)
        return _dot(vt_aug, p)

    outs = []
    prev = None
    for h in range(n_heads + 1):
        if h in hooks:
            hooks[h]()
        if h < n_heads:
            q_aug, rows = prepare(h)
            ts, m = [], None
        acc = None
        for i in range(j + 1):
            if h < n_heads:
                t = scores(h, q_aug, i)
                ts.append(t)
                mi = jnp.max(t, axis=0, keepdims=True) + rows[i]
                m = mi if m is None else jnp.maximum(m, mi)
            if prev is not None:
                p_ts, p_m, p_rows = prev
                part = values(h - 1, p_ts[i], p_m - p_rows[i], i)
                acc = part if acc is None else acc + part
        if prev is not None:
            outs.append(acc[:dh] / acc[dh:dh + 1])
        prev = (ts, m, rows) if h < n_heads else None
    o_ref[...] = jnp.concatenate(outs, axis=0).T.astype(o_ref.dtype)


def _sample_attention(n_pages, q_ref, kn_ref, vn_ref, slope_ref, kbuf, vbuf, o_ref):
    nq, _, d = q_ref.shape
    dh = d // N_HEADS
    rows = nq * N_HEADS
    pages_per_block = MOBA_BLOCK // PAGE_SIZE
    n_past = n_pages // pages_per_block
    past_len = n_pages * PAGE_SIZE

    lane_head = lax.broadcasted_iota(jnp.int32, (N_HEADS, d), 1) // dh
    head_mask = lane_head == lax.broadcasted_iota(jnp.int32, (N_HEADS, d), 0)
    qrows = jnp.concatenate([jnp.where(head_mask, q_ref[i], 0.0) for i in range(nq)], axis=0)
    qrows_b = qrows.astype(BF16)
    row_head_mask = jnp.concatenate([head_mask] * nq, axis=0)
    slope = slope_ref[...]
    row_q = lax.broadcasted_iota(jnp.int32, (rows, 1), 0) // N_HEADS

    scores = [_dot(qrows_b, kbuf[p].astype(BF16)) for p in range(n_pages)]
    gate = jnp.concatenate(
        [sum(jnp.sum(s, axis=1, keepdims=True) for s in scores[n * pages_per_block:(n + 1) * pages_per_block])
         for n in range(n_past)], axis=1) * (1.0 / MOBA_BLOCK)
    sel = _top_k_additive_mask(gate, n_past, 1)

    dist0 = (past_len + row_q).astype(F32)
    offs = lax.broadcasted_iota(jnp.int32, (1, PAGE_SIZE), 1)
    masked = []
    for p in range(n_pages):
        n = p // pages_per_block
        kpos = (p * PAGE_SIZE + offs).astype(F32)
        masked.append(scores[p] - slope * (dist0 - kpos) + sel[:, n:n + 1])
    own = []
    for t in range(nq):
        s = jnp.sum(qrows * kn_ref[t], axis=1, keepdims=True)
        s = s - slope * (row_q - t).astype(F32)
        own.append(jnp.where(row_q >= t, s, NEG_INF))
    m = own[0]
    for s in own[1:]:
        m = jnp.maximum(m, s)
    for s in masked:
        m = jnp.maximum(m, jnp.max(s, axis=1, keepdims=True))
    l = jnp.zeros((rows, 1), F32)
    acc = jnp.zeros((rows, d), F32)
    for t in range(nq):
        p_own = jnp.exp(own[t] - m)
        l = l + p_own
        acc = acc + p_own * vn_ref[t]
    for p in range(n_pages):
        pr = jnp.exp(masked[p] - m)
        l = l + jnp.sum(pr, axis=1, keepdims=True)
        acc = acc + _dot_nt(pr.astype(BF16), vbuf[p].astype(BF16))
    o_rows = jnp.where(row_head_mask, acc / l, 0.0)
    o = jnp.sum(o_rows.reshape(nq, N_HEADS, d), axis=1)
    for t in range(nq):
        o_ref[t] = o[t:t + 1, :]


def _attn_kernel(j, fused, seq_base, n_pages, pt_ref, slopes_ref, qt_ref, kb_ref, vt_ref, kmean_ref,
                 o_in_ref, *rest):
    if not fused:
        (o_ref,) = rest
        _prompt_attention(j, slopes_ref, qt_ref, kb_ref, vt_ref, kmean_ref, o_ref, {})
        return
    qs_ref, kn_ref, vn_ref, srow_ref, ck_ref, cv_ref, os_in_ref, o_ref, os_ref, kbuf, vbuf, sem = rest
    n_steps = pl.num_programs(0) * pl.num_programs(1)
    step = pl.program_id(0) * pl.num_programs(1) + pl.program_id(1)
    slot = step % 2

    def page_copies(at_step, at_slot):
        seq = seq_base + at_step
        copies = []
        for i in range(n_pages):
            page = pt_ref[seq * n_pages + i]
            copies.append(pltpu.make_async_copy(ck_ref.at[page], kbuf.at[at_slot, i], sem.at[at_slot, 0, i]))
            copies.append(pltpu.make_async_copy(cv_ref.at[page], vbuf.at[at_slot, i], sem.at[at_slot, 1, i]))
        return copies

    @pl.when(step == 0)
    def _():
        for c in page_copies(step, slot):
            c.start()

    @pl.when(step < n_steps)
    def _():
        for c in page_copies(step, slot):
            c.wait()

    @pl.when(step + 1 < n_steps)
    def _():
        for c in page_copies(step + 1, 1 - slot):
            c.start()

    _prompt_attention(j, slopes_ref, qt_ref, kb_ref, vt_ref, kmean_ref, o_ref, {})
    _sample_attention(n_pages, qs_ref, kn_ref, vn_ref, srow_ref, kbuf.at[slot], vbuf.at[slot], os_ref)


def _attention(qt, kb, vt, kmean, slopes, page_table, q_s, k_s, v_s, slope_rows, cache_k, cache_v):
    batch, d, seq = qt.shape
    nq, n_seq, _ = q_s.shape
    blk = MOBA_BLOCK
    n_blocks = seq // blk
    dh = d // N_HEADS
    n_pages = page_table.shape[1]
    pool_pages = cache_k.shape[0]
    ck = cache_k.transpose(0, 2, 3, 1).reshape(pool_pages, d, PAGE_SIZE)
    cv = cache_v.transpose(0, 2, 3, 1).reshape(pool_pages, d, PAGE_SIZE)
    per_seq = lambda a: a.reshape(nq, n_seq, 1, d)
    fused_steps = batch * (N_HEADS // ATTN_HEADS_PER_FUSED_STEP)
    assert n_seq % fused_steps == 0 and n_seq // fused_steps <= n_blocks
    first_fused = n_blocks - n_seq // fused_steps
    pt = page_table.reshape(-1)
    out = jnp.zeros((batch, seq, d), BF16)
    out_s = jnp.zeros((nq, n_seq, 1, d), F32)
    for j in range(n_blocks):
        keys = (j + 1) * blk
        fused = j >= first_fused
        width = (ATTN_HEADS_PER_FUSED_STEP if fused else ATTN_HEADS_PER_STEP) * dh
        groups = d // width
        seq_base = (j - first_fused) * fused_steps
        in_specs = [
            pl.BlockSpec(memory_space=pltpu.SMEM),
            pl.BlockSpec((None, width, blk), lambda b, g, pt, j=j: (b, g, j)),
            pl.BlockSpec((None, keys, width), lambda b, g, pt: (b, 0, g)),
            pl.BlockSpec((None, width, keys), lambda b, g, pt: (b, g, 0)),
            pl.BlockSpec((None, n_blocks, width), lambda b, g, pt: (b, 0, g)),
            pl.BlockSpec(memory_space=pl.ANY),
        ]
        args = [pt, slopes, qt, kb, vt, kmean, out]
        out_specs = [pl.BlockSpec((None, blk, width), lambda b, g, pt, j=j: (b, j, g))]
        out_shape = [jax.ShapeDtypeStruct((batch, seq, d), BF16)]
        scratch = []
        aliases = {len(args) - 1: 0}
        if fused:
            seq_spec = pl.BlockSpec((nq, None, 1, d),
                                    lambda b, g, pt, s0=seq_base, groups=groups: (0, s0 + b * groups + g, 0, 0))
            in_specs += [seq_spec, seq_spec, seq_spec,
                         pl.BlockSpec(slope_rows.shape, lambda b, g, pt: (0, 0)),
                         pl.BlockSpec(memory_space=pl.ANY), pl.BlockSpec(memory_space=pl.ANY),
                         pl.BlockSpec(memory_space=pl.ANY)]
            args += [per_seq(q_s), per_seq(k_s), per_seq(v_s), slope_rows, ck, cv, out_s]
            aliases[len(args) - 1] = 1
            out_specs.append(seq_spec)
            out_shape.append(jax.ShapeDtypeStruct((nq, n_seq, 1, d), F32))
            scratch = [pltpu.VMEM((2, n_pages, d, PAGE_SIZE), F32), pltpu.VMEM((2, n_pages, d, PAGE_SIZE), F32),
                       pltpu.SemaphoreType.DMA((2, 2, n_pages))]
        res = pl.pallas_call(
            functools.partial(_attn_kernel, j, fused, seq_base, n_pages),
            grid_spec=pltpu.PrefetchScalarGridSpec(
                num_scalar_prefetch=1, grid=(batch, groups), in_specs=in_specs, out_specs=out_specs,
                scratch_shapes=scratch),
            out_shape=out_shape,
            input_output_aliases=aliases,
            compiler_params=_params(2),
            name=f"attn_{j}",
        )(*args)
        out = res[0]
        if fused:
            out_s = res[1]
    return out, out_s


def kernel(x_prompt, x_sample, c_prompt, c_sample, cache_k, cache_v, page_table, ada_w, ada_b, norm1_g, norm2_g,
           ffn_w_in, ffn_w_out, sgu_w_in, sgu_v_g, sgu_w_s, sgu_b_s, sgu_w_out, kv_ada_w, kv_ada_b, kv_norm_g,
           w_kv, k_norm_g, attn_w_q, q_norm_g, attn_w_o):
    batch, seq, d = x_prompt.shape
    n_seq, nq, _ = x_sample.shape
    dh = d // N_HEADS
    gd = d // N_SGU_GROUPS
    n_tok_p = batch * seq
    n_tok_s = n_seq * nq
    assert seq % TOKEN_TILE == 0 and n_tok_s == TOKEN_TILE and n_seq % SUBLANES == 0 and batch <= SUBLANES

    c_all = jnp.concatenate([c_sample, c_prompt, jnp.zeros((SUBLANES - batch, d), F32)], axis=0)
    mods = _ada(c_all, ada_w, ada_b, 1536)
    kvmods = _ada(c_all, kv_ada_w[None], kv_ada_b[None], 1024)

    bf = lambda w: w.astype(BF16)
    sgu_w_in_b, sgu_w_out_b = bf(sgu_w_in[0]), bf(sgu_w_out[0])
    ffn_in_b, ffn_out_b = bf(ffn_w_in), bf(ffn_w_out)
    w_k_b, w_v_b = bf(w_kv[:, :d]), bf(w_kv[:, d:])
    w_q_b, w_qt_b, w_o_b = bf(attn_w_q[0]), bf(attn_w_q[0].T), bf(attn_w_o[0])
    k_gain = jnp.tile(k_norm_g, N_HEADS).reshape(1, d)
    q_gain = (jnp.tile(q_norm_g[0], N_HEADS) * (dh ** -0.5)).reshape(1, d)
    kv_norm_row = kv_norm_g.reshape(1, d)
    head_of_lane = np.arange(d) // dh
    pool = jnp.asarray((head_of_lane[:, None] == np.arange(LANES)[None, :]) / dh, BF16)
    expand = jnp.asarray(np.arange(LANES)[:, None] == head_of_lane[None, :], BF16)
    slopes = jnp.exp2(-8.0 * jnp.arange(1, N_HEADS + 1, dtype=F32) / N_HEADS)

    wmix_p = sgu_w_s[0]
    bmix_p = jnp.repeat(sgu_b_s[0].T, gd, axis=1)
    wcoef_s = jnp.repeat(sgu_w_s[0][:, :nq, :nq].transpose(1, 2, 0).reshape(nq * nq, N_SGU_GROUPS), gd, axis=1)
    bcoef_s = jnp.repeat(sgu_b_s[0][:, :nq].T, gd, axis=1)

    xp = x_prompt.reshape(n_tok_p, d)
    xs = x_sample.transpose(1, 0, 2).reshape(n_tok_s, d)
    tiles_pb = seq // TOKEN_TILE

    uz_p, sguv_p = _sgu_prompt(xp, batch, mods, n_seq, norm1_g, sgu_w_in_b, sgu_v_g, wmix_p, bmix_p)
    x2_p = _ffn(xp, uz_p, 0, mods, n_seq, tiles_pb, sgu_w_out_b, norm2_g, ffn_in_b, ffn_out_b)
    kt_p, vt_p, kb_p, vtb_p, qt_p, kmean_p = _kvq_prompt(
        x2_p, batch, kvmods, mods, n_seq, kv_norm_row, norm1_g, bf(w_kv[:, :d].T), bf(w_kv[:, d:].T), w_qt_b,
        k_gain.reshape(d, 1), (q_gain * LOG2_E).reshape(d, 1))
    uz_s, sguv_s = _sgu_sample(xs, nq, mods, n_seq, norm1_g, sgu_w_in_b, sgu_v_g, wcoef_s, bcoef_s)
    x2_s = _ffn(xs, uz_s, 0, mods, n_seq, 0, sgu_w_out_b, norm2_g, ffn_in_b, ffn_out_b)
    k_s, v_s, q_s, kt_s, vt_s = _kvq_sample(x2_s, nq, kvmods, mods, n_seq, kv_norm_row, norm1_g, w_k_b, w_v_b,
                                            w_q_b, k_gain, q_gain, pool, expand)

    slope_rows = jnp.tile(slopes, nq).reshape(nq * N_HEADS, 1)
    o_p, o_s = _attention(qt_p, kb_p.reshape(batch, seq, d), vtb_p, kmean_p.reshape(batch, seq // MOBA_BLOCK, d),
                          slopes * LOG2_E, page_table, q_s.reshape(nq, n_seq, d), k_s.reshape(nq, n_seq, d),
                          v_s.reshape(nq, n_seq, d), slope_rows, cache_k, cache_v)
    y_p = _ffn(x2_p, o_p.reshape(n_tok_p, d), 1, mods, n_seq, tiles_pb, w_o_b, norm2_g, ffn_in_b, ffn_out_b)
    y_s = _ffn(x2_s, o_s.reshape(n_tok_s, d), 1, mods, n_seq, 0, w_o_b, norm2_g, ffn_in_b, ffn_out_b)

    by_seq = lambda a: a.reshape(nq, n_seq, d).transpose(1, 0, 2)
    return (y_p.reshape(batch, seq, d), by_seq(y_s),
            kt_p.reshape(batch, N_HEADS, dh, seq).transpose(0, 3, 1, 2),
            vt_p.reshape(batch, N_HEADS, dh, seq).transpose(0, 3, 1, 2),
            kt_s.reshape(nq, N_HEADS, dh, n_seq).transpose(3, 0, 1, 2),
            vt_s.reshape(nq, N_HEADS, dh, n_seq).transpose(3, 0, 1, 2),
            sguv_p.reshape(1, batch, CHUNK, d), by_seq(sguv_s)[None])
```

```python
import functools

import numpy as np
import jax
import jax.numpy as jnp
from jax import lax
from jax.experimental import pallas as pl
from jax.experimental.pallas import tpu as pltpu

F32 = jnp.float32
BF16 = jnp.bfloat16
EPS = 1e-6
NEG_INF = float("-inf")
LOG2_E = 1.4426950408889634

V7X_VMEM_LIMIT_BYTES = 56 * 1024 * 1024
LANES = 128
SUBLANES = 8

N_HEADS = 16
MOBA_BLOCK = 256
MOBA_TOP_K = 3
CHUNK = 128
N_SGU_GROUPS = 8
PAGE_SIZE = 128
TOKEN_TILE = 512
ATTN_HEADS_PER_STEP = 8


def _params(n_grid_dims):
    return pltpu.CompilerParams(
        dimension_semantics=("arbitrary",) * n_grid_dims,
        vmem_limit_bytes=V7X_VMEM_LIMIT_BYTES)


def _const_spec(shape):
    n = len(shape)
    return pl.BlockSpec(shape, lambda *_: (0,) * n, pipeline_mode=pl.Buffered(1))


def _layer_spec(w, layer):
    tail = (0,) * (w.ndim - 1)
    return pl.BlockSpec((None,) + w.shape[1:], lambda *_: (layer,) + tail, pipeline_mode=pl.Buffered(1))


def _mod_spec(mods, layer, n_seq, tiles_per_batch):
    w = mods.shape[-1]
    if tiles_per_batch:
        return pl.BlockSpec((None, SUBLANES, w), lambda *_: (layer, n_seq // SUBLANES, 0))
    return pl.BlockSpec((None, n_seq, w), lambda *_: (layer, 0, 0))


def _mod_rows(mod_ref, tiles_per_batch):
    if tiles_per_batch:
        return mod_ref[pl.ds(pl.program_id(0) // tiles_per_batch, 1), :]
    return mod_ref[...]


def _per_token(v, tm):
    if v.shape[0] in (1, tm):
        return v
    return jnp.concatenate([v] * (tm // v.shape[0]), axis=0)


def _dot(a, b):
    return jnp.dot(a, b, preferred_element_type=F32)


def _dot_nt(a, b):
    return lax.dot_general(a, b, (((1,), (1,)), ((), ())), preferred_element_type=F32)


def _split_bf16(x):
    hi = x.astype(BF16)
    lo = (x - hi.astype(F32)).astype(BF16)
    return hi, lo


def _rms(x, g):
    return x * lax.rsqrt(jnp.mean(x * x, axis=-1, keepdims=True) + EPS) * g


def _modulate(x, g, shift, scale):
    tm = x.shape[0]
    return _rms(x, g) * (1.0 + _per_token(scale, tm)) + _per_token(shift, tm)


def _head_rms(x, gain, pool_ref, expand_ref):
    sq_hi, sq_lo = _split_bf16(x * x)
    pool = pool_ref[...]
    ms = _dot(sq_hi, pool) + _dot(sq_lo, pool)
    r_hi, r_lo = _split_bf16(lax.rsqrt(ms + EPS))
    expand = expand_ref[...]
    return x * (_dot(r_hi, expand) + _dot(r_lo, expand)) * gain


def _head_rms_t(xt, gain_col):
    d, tm = xt.shape
    x3 = xt.reshape(N_HEADS, d // N_HEADS, tm)
    x3 = x3 * lax.rsqrt(jnp.mean(x3 * x3, axis=1, keepdims=True) + EPS)
    return x3.reshape(d, tm) * gain_col


def _ada_kernel(c_ref, w_ref, b_ref, o_ref):
    c = c_ref[...]
    s = (c * jax.nn.sigmoid(c)).astype(BF16)
    o_ref[...] = _dot(s, w_ref[...].astype(BF16)) + b_ref[pl.ds(pl.program_id(0), 1), :]


def _ada(c_all, w, b, tn):
    n_layers, d, n = w.shape
    m = c_all.shape[0]
    return pl.pallas_call(
        _ada_kernel,
        grid=(n_layers, n // tn),
        in_specs=[
            pl.BlockSpec((m, d), lambda l, j: (0, 0)),
            pl.BlockSpec((None, d, tn), lambda l, j: (l, 0, j)),
            pl.BlockSpec((n_layers, tn), lambda l, j: (0, j)),
        ],
        out_specs=pl.BlockSpec((None, m, tn), lambda l, j: (l, 0, j)),
        out_shape=jax.ShapeDtypeStruct((n_layers, m, n), F32),
        compiler_params=_params(2),
        name="ada",
    )(c_all, w, b)


def _sgu_front(x_ref, mod, g_ref, win_ref, vg_ref):
    d = x_ref.shape[1]
    h = _modulate(x_ref[...], g_ref[0:1, :], mod[:, 0:d], mod[:, d:2 * d]).astype(BF16)
    uv = jax.nn.gelu(_dot(h, win_ref[...]))
    return uv[:, :d], _rms(uv[:, d:], vg_ref[...])


def _sgu_prompt_kernel(tiles_per_batch, x_ref, mod_ref, g_ref, win_ref, vg_ref, wmix_ref, bmix_ref, uz_ref, v_ref):
    tm, d = x_ref.shape
    c_rows = wmix_ref.shape[1]
    gd = d // N_SGU_GROUPS
    u, v = _sgu_front(x_ref, _mod_rows(mod_ref, tiles_per_batch), g_ref, win_ref, vg_ref)
    v_ref[...] = v[tm - v_ref.shape[0]:, :]
    vb = v.astype(BF16)
    causal = (lax.broadcasted_iota(jnp.int32, (c_rows, c_rows), 1)
              <= lax.broadcasted_iota(jnp.int32, (c_rows, c_rows), 0))
    for g in range(N_SGU_GROUPS):
        wg = jnp.where(causal, wmix_ref[g], 0.0).astype(BF16)
        cols = slice(g * gd, (g + 1) * gd)
        for ch in range(tm // c_rows):
            rows = slice(ch * c_rows, (ch + 1) * c_rows)
            mixed = _dot(wg, vb[rows, cols]) + bmix_ref[:, cols]
            uz_ref[rows, cols] = (u[rows, cols] * mixed).astype(BF16)


def _sgu_prompt(x, batch, mods, n_seq, norm_g, w_in, v_g, wmix, bmix):
    t, d = x.shape
    tm = TOKEN_TILE
    tiles_per_batch = t // batch // tm
    return pl.pallas_call(
        functools.partial(_sgu_prompt_kernel, tiles_per_batch),
        grid=(t // tm,),
        in_specs=[
            pl.BlockSpec((tm, d), lambda i: (i, 0)),
            _mod_spec(mods, 0, n_seq, tiles_per_batch),
            _const_spec(norm_g.shape),
            _const_spec(w_in.shape),
            _const_spec(v_g.shape),
            _const_spec(wmix.shape),
            _const_spec(bmix.shape),
        ],
        out_specs=[pl.BlockSpec((tm, d), lambda i: (i, 0)),
                   pl.BlockSpec((None, CHUNK, d), lambda i: (i // tiles_per_batch, 0, 0))],
        out_shape=[jax.ShapeDtypeStruct((t, d), BF16), jax.ShapeDtypeStruct((batch, CHUNK, d), F32)],
        compiler_params=_params(1),
        name="sgu_prompt",
    )(x, mods, norm_g, w_in, v_g, wmix, bmix)


def _sgu_sample_kernel(nq, x_ref, mod_ref, g_ref, win_ref, vg_ref, wcoef_ref, bcoef_ref, uz_ref, v_ref):
    tm = x_ref.shape[0]
    ns = tm // nq
    u, v = _sgu_front(x_ref, mod_ref[...], g_ref, win_ref, vg_ref)
    v_ref[...] = v
    for t in range(nq):
        mixed = bcoef_ref[t:t + 1, :]
        for s in range(t + 1):
            mixed = mixed + wcoef_ref[t * nq + s:t * nq + s + 1, :] * v[s * ns:(s + 1) * ns, :]
        uz_ref[t * ns:(t + 1) * ns, :] = (u[t * ns:(t + 1) * ns, :] * mixed).astype(BF16)


def _sgu_sample(x, nq, mods, n_seq, norm_g, w_in, v_g, wcoef, bcoef):
    t, d = x.shape
    full = pl.BlockSpec((t, d), lambda i: (0, 0))
    return pl.pallas_call(
        functools.partial(_sgu_sample_kernel, nq),
        grid=(1,),
        in_specs=[
            full,
            _mod_spec(mods, 0, n_seq, 0),
            _const_spec(norm_g.shape),
            _const_spec(w_in.shape),
            _const_spec(v_g.shape),
            _const_spec(wcoef.shape),
            _const_spec(bcoef.shape),
        ],
        out_specs=[full, full],
        out_shape=[jax.ShapeDtypeStruct((t, d), BF16), jax.ShapeDtypeStruct((t, d), F32)],
        compiler_params=_params(1),
        name="sgu_sample",
    )(x, mods, norm_g, w_in, v_g, wcoef, bcoef)


def _ff_chunks(d_ff):
    step = 768
    return tuple((s, min(step, d_ff - s)) for s in range(0, d_ff, step))


def _ffn_kernel(layer, tiles_per_batch, x_ref, z_ref, mod_ref, wp_ref, g_ref, win_ref, wout_ref, o_ref):
    tm, d = x_ref.shape
    d_ff = wout_ref.shape[0]
    mod = _mod_rows(mod_ref, tiles_per_batch)
    gate1 = _per_token(mod[:, 2 * d:3 * d], tm)
    gate2 = _per_token(mod[:, 5 * d:6 * d], tm)
    x1 = x_ref[...] + gate1 * _dot(z_ref[...].astype(BF16), wp_ref[...])
    h = _modulate(x1, g_ref[layer:layer + 1, :], mod[:, 3 * d:4 * d], mod[:, 4 * d:5 * d]).astype(BF16)
    acc = None
    for s, w in _ff_chunks(d_ff):
        a = _dot(h, win_ref[:, s:s + w])
        b = _dot(h, win_ref[:, d_ff + s:d_ff + s + w])
        act = (a * jax.nn.sigmoid(a) * b).astype(BF16)
        part = _dot(act, wout_ref[s:s + w, :])
        acc = part if acc is None else acc + part
    o_ref[...] = x1 + gate2 * acc


def _ffn(x, z, layer, mods, n_seq, tiles_per_batch, w_proj, norm_g, w_in, w_out):
    t, d = x.shape
    tm = TOKEN_TILE
    return pl.pallas_call(
        functools.partial(_ffn_kernel, layer, tiles_per_batch),
        grid=(t // tm,),
        in_specs=[
            pl.BlockSpec((tm, d), lambda i: (i, 0)),
            pl.BlockSpec((tm, d), lambda i: (i, 0)),
            _mod_spec(mods, layer, n_seq, tiles_per_batch),
            _const_spec(w_proj.shape),
            _const_spec(norm_g.shape),
            _layer_spec(w_in, layer),
            _layer_spec(w_out, layer),
        ],
        out_specs=pl.BlockSpec((tm, d), lambda i: (i, 0)),
        out_shape=jax.ShapeDtypeStruct((t, d), F32),
        compiler_params=_params(1),
        name="ffn",
    )(x, z, mods, w_proj, norm_g, w_in, w_out)


def _kvq_prompt_kernel(tiles_per_batch, x_ref, kvmod_ref, mod_ref, kvg_ref, n1g_ref, wkt_ref, wvt_ref, wqt_ref,
                       kg_ref, qg_ref, kt_ref, vt_ref, kb_ref, vtb_ref, qt_ref, kmean_ref):
    tm, d = x_ref.shape
    x = x_ref[...]
    kvmod = _mod_rows(kvmod_ref, tiles_per_batch)
    mod = _mod_rows(mod_ref, tiles_per_batch)
    hkv = _modulate(x, kvg_ref[...], kvmod[:, 0:d], kvmod[:, d:2 * d]).astype(BF16)
    kt = _head_rms_t(_dot_nt(wkt_ref[...], hkv), kg_ref[...])
    vt = _dot_nt(wvt_ref[...], hkv)
    kt_ref[...] = kt
    vt_ref[...] = vt
    vtb_ref[...] = vt.astype(BF16)
    hq = _modulate(x, n1g_ref[1:2, :], mod[:, 0:d], mod[:, d:2 * d]).astype(BF16)
    qt_ref[...] = _head_rms_t(_dot_nt(wqt_ref[...], hq), qg_ref[...]).astype(BF16)
    for i in range(tm // MOBA_BLOCK):
        cols = slice(i * MOBA_BLOCK, (i + 1) * MOBA_BLOCK)
        k_nat = kt[:, cols].T
        kmean_ref[i] = jnp.mean(k_nat, axis=0, keepdims=True)
        kb_ref[cols, :] = k_nat.astype(BF16)


def _kvq_prompt(x, batch, kvmods, mods, n_seq, kv_norm_g, norm1_g, w_kt, w_vt, w_qt, k_gain_col, q_gain_col):
    t, d = x.shape
    tm = TOKEN_TILE
    seq = t // batch
    tiles_per_batch = seq // tm
    nblk = tm // MOBA_BLOCK
    row_spec = pl.BlockSpec((tm, d), lambda i: (i, 0))
    col_spec = pl.BlockSpec((None, d, tm), lambda i: (i // tiles_per_batch, 0, i % tiles_per_batch))
    return pl.pallas_call(
        functools.partial(_kvq_prompt_kernel, tiles_per_batch),
        grid=(t // tm,),
        in_specs=[
            row_spec,
            _mod_spec(kvmods, 0, n_seq, tiles_per_batch),
            _mod_spec(mods, 1, n_seq, tiles_per_batch),
            _const_spec(kv_norm_g.shape), _const_spec(norm1_g.shape),
            _const_spec(w_kt.shape), _const_spec(w_vt.shape), _const_spec(w_qt.shape),
            _const_spec((d, 1)), _const_spec((d, 1)),
        ],
        out_specs=[
            col_spec, col_spec, row_spec, col_spec, col_spec,
            pl.BlockSpec((nblk, 1, d), lambda i: (i, 0, 0)),
        ],
        out_shape=[
            jax.ShapeDtypeStruct((batch, d, seq), F32),
            jax.ShapeDtypeStruct((batch, d, seq), F32),
            jax.ShapeDtypeStruct((t, d), BF16),
            jax.ShapeDtypeStruct((batch, d, seq), BF16),
            jax.ShapeDtypeStruct((batch, d, seq), BF16),
            jax.ShapeDtypeStruct((t // MOBA_BLOCK, 1, d), F32),
        ],
        compiler_params=_params(1),
        name="kvq_prompt",
    )(x, kvmods, mods, kv_norm_g, norm1_g, w_kt, w_vt, w_qt, k_gain_col, q_gain_col)


def _kvq_sample_kernel(nq, x_ref, kvmod_ref, mod_ref, kvg_ref, n1g_ref, wk_ref, wv_ref, wq_ref, kg_ref, qg_ref,
                       pool_ref, expand_ref, k_ref, v_ref, q_ref, kt_ref, vt_ref):
    tm, d = x_ref.shape
    ns = tm // nq
    x = x_ref[...]
    kvmod = kvmod_ref[...]
    mod = mod_ref[...]
    hkv = _modulate(x, kvg_ref[...], kvmod[:, 0:d], kvmod[:, d:2 * d]).astype(BF16)
    k = _head_rms(_dot(hkv, wk_ref[...]), kg_ref[...], pool_ref, expand_ref)
    v = _dot(hkv, wv_ref[...])
    k_ref[...] = k
    v_ref[...] = v
    for t in range(nq):
        kt_ref[t] = k[t * ns:(t + 1) * ns, :].T
        vt_ref[t] = v[t * ns:(t + 1) * ns, :].T
    hq = _modulate(x, n1g_ref[1:2, :], mod[:, 0:d], mod[:, d:2 * d]).astype(BF16)
    q_ref[...] = _head_rms(_dot(hq, wq_ref[...]), qg_ref[...], pool_ref, expand_ref)


def _kvq_sample(x, nq, kvmods, mods, n_seq, kv_norm_g, norm1_g, w_k, w_v, w_q, k_gain, q_gain, pool, expand):
    t, d = x.shape
    full = pl.BlockSpec((t, d), lambda i: (0, 0))
    tspec = pl.BlockSpec((nq, d, t // nq), lambda i: (0, 0, 0))
    return pl.pallas_call(
        functools.partial(_kvq_sample_kernel, nq),
        grid=(1,),
        in_specs=[
            full,
            _mod_spec(kvmods, 0, n_seq, 0),
            _mod_spec(mods, 1, n_seq, 0),
            _const_spec(kv_norm_g.shape), _const_spec(norm1_g.shape),
            _const_spec(w_k.shape), _const_spec(w_v.shape), _const_spec(w_q.shape),
            _const_spec((1, d)), _const_spec((1, d)),
            _const_spec(pool.shape), _const_spec(expand.shape),
        ],
        out_specs=[full, full, full, tspec, tspec],
        out_shape=[jax.ShapeDtypeStruct((t, d), F32)] * 3 + [jax.ShapeDtypeStruct((nq, d, t // nq), F32)] * 2,
        compiler_params=_params(1),
        name="kvq_sample",
    )(x, kvmods, mods, kv_norm_g, norm1_g, w_k, w_v, w_q, k_gain, q_gain, pool, expand)


def _top_k_additive_mask(gate, own, axis):
    nb = gate.shape[axis]
    idx = lax.broadcasted_iota(jnp.int32, gate.shape, axis)
    eligible = idx < own
    g = jnp.where(eligible, gate, NEG_INF)
    mask = jnp.full(gate.shape, NEG_INF, F32)
    for _ in range(min(MOBA_TOP_K, nb)):
        best = jnp.max(g, axis=axis, keepdims=True)
        first = jnp.min(jnp.where(g == best, idx, nb), axis=axis, keepdims=True)
        pick = idx == first
        mask = jnp.where(pick, 0.0, mask)
        g = jnp.where(pick, NEG_INF, g)
    return jnp.where(eligible, mask, NEG_INF)


def _prompt_attention(j, slopes_ref, qt_ref, kb_ref, vt_ref, kmean_ref, o_ref, hooks):
    group = pl.program_id(1)
    blk = MOBA_BLOCK
    pair_w = LANES
    n_heads = 2 * (qt_ref.shape[0] // pair_w)
    dh = pair_w // 2
    zeros = jnp.zeros((dh, blk), BF16)
    causal = (lax.broadcasted_iota(jnp.int32, (blk, blk), 0)
              <= lax.broadcasted_iota(jnp.int32, (blk, blk), 1))
    n_feat = 3
    k_feat = jnp.where(lax.broadcasted_iota(jnp.int32, (blk, LANES), 1) < n_feat,
                       lax.broadcasted_iota(jnp.int32, (blk, LANES), 0), 0).astype(F32).astype(BF16)
    feat_row = lax.broadcasted_iota(jnp.int32, (LANES, blk), 0)
    ones_rows = jnp.ones((16, blk), BF16)

    def prepare(h):
        lanes = slice((h // 2) * pair_w, (h // 2 + 1) * pair_w)
        slope = slopes_ref[group * n_heads + h]
        qt = qt_ref[lanes, :]
        qh = jnp.concatenate([qt[:dh], zeros] if h % 2 == 0 else [zeros, qt[dh:]], axis=0)
        q_feat = jnp.zeros((LANES, blk), F32)
        rest_slope = jnp.full((1, blk), slope, F32)
        for f in range(n_feat):
            piece = rest_slope.astype(BF16).astype(F32)
            q_feat = jnp.where(feat_row == f, piece, q_feat)
            rest_slope = rest_slope - piece
        q_aug = jnp.concatenate([qh, q_feat.astype(BF16)], axis=0)
        rows = []
        if j > 0:
            km_hi, km_lo = _split_bf16(kmean_ref[:, lanes])
            sel = _top_k_additive_mask(_dot(km_hi, qh) + _dot(km_lo, qh), j, 0)
            rows = [sel[i:i + 1, :] + slope * float((i - j) * blk) for i in range(j)]
        rows.append(jnp.zeros((1, blk), F32))
        return q_aug, rows

    def scores(h, q_aug, i):
        lanes = slice((h // 2) * pair_w, (h // 2 + 1) * pair_w)
        t = _dot(jnp.concatenate([kb_ref[i * blk:(i + 1) * blk, lanes], k_feat], axis=1), q_aug)
        return jnp.where(causal, t, NEG_INF) if i == j else t

    def values(h, t, shift, i):
        p = jnp.exp2((t - shift).astype(BF16))
        vt_aug = jnp.concatenate([vt_ref[h * dh:(h + 1) * dh, i * blk:(i + 1) * blk], ones_rows], axis=0)
        return _dot(vt_aug, p)

    outs = []
    prev = None
    for h in range(n_heads + 1):
        if h in hooks:
            hooks[h]()
        if h < n_heads:
            q_aug, rows = prepare(h)
            ts, m = [], None
        acc = None
        for i in range(j + 1):
            if h < n_heads:
                t = scores(h, q_aug, i)
                ts.append(t)
                mi = jnp.max(t, axis=0, keepdims=True) + rows[i]
                m = mi if m is None else jnp.maximum(m, mi)
            if prev is not None:
                p_ts, p_m, p_rows = prev
                part = values(h - 1, p_ts[i], p_m - p_rows[i], i)
                acc = part if acc is None else acc + part
        if prev is not None:
            outs.append(acc[:dh] / acc[dh:dh + 1])
        prev = (ts, m, rows) if h < n_heads else None
    o_ref[...] = jnp.concatenate(outs, axis=0).T.astype(o_ref.dtype)


def _sample_attention(n_pages, q_ref, kn_ref, vn_ref, slope_ref, kbuf, vbuf, o_ref):
    nq, _, d = q_ref.shape
    dh = d // N_HEADS
    rows = nq * N_HEADS
    pages_per_block = MOBA_BLOCK // PAGE_SIZE
    n_past = n_pages // pages_per_block
    past_len = n_pages * PAGE_SIZE

    lane_head = lax.broadcasted_iota(jnp.int32, (N_HEADS, d), 1) // dh
    head_mask = lane_head == lax.broadcasted_iota(jnp.int32, (N_HEADS, d), 0)
    qrows = jnp.concatenate([jnp.where(head_mask, q_ref[i], 0.0) for i in range(nq)], axis=0)
    qrows_b = qrows.astype(BF16)
    row_head_mask = jnp.concatenate([head_mask] * nq, axis=0)
    slope = slope_ref[...]
    row_q = lax.broadcasted_iota(jnp.int32, (rows, 1), 0) // N_HEADS

    scores = [_dot(qrows_b, kbuf[p].astype(BF16)) for p in range(n_pages)]
    gate = jnp.concatenate(
        [sum(jnp.sum(s, axis=1, keepdims=True) for s in scores[n * pages_per_block:(n + 1) * pages_per_block])
         for n in range(n_past)], axis=1) * (1.0 / MOBA_BLOCK)
    sel = _top_k_additive_mask(gate, n_past, 1)

    dist0 = (past_len + row_q).astype(F32)
    offs = lax.broadcasted_iota(jnp.int32, (1, PAGE_SIZE), 1)
    masked = []
    for p in range(n_pages):
        n = p // pages_per_block
        kpos = (p * PAGE_SIZE + offs).astype(F32)
        masked.append(scores[p] - slope * (dist0 - kpos) + sel[:, n:n + 1])
    own = []
    for t in range(nq):
        s = jnp.sum(qrows * kn_ref[t], axis=1, keepdims=True)
        s = s - slope * (row_q - t).astype(F32)
        own.append(jnp.where(row_q >= t, s, NEG_INF))
    m = own[0]
    for s in own[1:]:
        m = jnp.maximum(m, s)
    for s in masked:
        m = jnp.maximum(m, jnp.max(s, axis=1, keepdims=True))
    l = jnp.zeros((rows, 1), F32)
    acc = jnp.zeros((rows, d), F32)
    for t in range(nq):
        p_own = jnp.exp(own[t] - m)
        l = l + p_own
        acc = acc + p_own * vn_ref[t]
    for p in range(n_pages):
        pr = jnp.exp(masked[p] - m)
        l = l + jnp.sum(pr, axis=1, keepdims=True)
        acc = acc + _dot_nt(pr.astype(BF16), vbuf[p].astype(BF16))
    o_rows = jnp.where(row_head_mask, acc / l, 0.0)
    o = jnp.sum(o_rows.reshape(nq, N_HEADS, d), axis=1)
    for t in range(nq):
        o_ref[t] = o[t:t + 1, :]


def _attn_kernel(j, fused, seq_base, n_pages, pt_ref, slopes_ref, qt_ref, kb_ref, vt_ref, kmean_ref,
                 o_in_ref, *rest):
    if not fused:
        (o_ref,) = rest
        _prompt_attention(j, slopes_ref, qt_ref, kb_ref, vt_ref, kmean_ref, o_ref, {})
        return
    qs_ref, kn_ref, vn_ref, srow_ref, ck_ref, cv_ref, os_in_ref, o_ref, os_ref, kbuf, vbuf, sem = rest
    n_steps = pl.num_programs(0) * pl.num_programs(1)
    step = pl.program_id(0) * pl.num_programs(1) + pl.program_id(1)
    slot = step % 2

    def page_copies(at_step, at_slot):
        seq = seq_base + at_step
        copies = []
        for i in range(n_pages):
            page = pt_ref[seq * n_pages + i]
            copies.append(pltpu.make_async_copy(ck_ref.at[page], kbuf.at[at_slot, i], sem.at[at_slot, 0, i]))
            copies.append(pltpu.make_async_copy(cv_ref.at[page], vbuf.at[at_slot, i], sem.at[at_slot, 1, i]))
        return copies

    @pl.when(step == 0)
    def _():
        for c in page_copies(step, slot):
            c.start()

    @pl.when(step < n_steps)
    def _():
        for c in page_copies(step, slot):
            c.wait()

    @pl.when(step + 1 < n_steps)
    def _():
        for c in page_copies(step + 1, 1 - slot):
            c.start()

    _prompt_attention(j, slopes_ref, qt_ref, kb_ref, vt_ref, kmean_ref, o_ref, {})
    _sample_attention(n_pages, qs_ref, kn_ref, vn_ref, srow_ref, kbuf.at[slot], vbuf.at[slot], os_ref)


def _attn_plan(n_blocks, batch, d, n_seq, n_pages):
    page_bufs = 2 * 2 * n_pages * d * PAGE_SIZE * 4
    other = 3 * 1024 * 1024
    plan = [(ATTN_HEADS_PER_STEP, None)] * n_blocks
    placed = 0
    for j in reversed(range(n_blocks)):
        if placed == n_seq:
            break
        heads = ATTN_HEADS_PER_STEP
        while heads > 2:
            keys = (j + 1) * MOBA_BLOCK
            blocks = 2 * 2 * keys * heads * (d // N_HEADS) * 2
            tiles = 2 * (j + 1) * MOBA_BLOCK * MOBA_BLOCK * 4
            if page_bufs + blocks + tiles + other <= V7X_VMEM_LIMIT_BYTES - 2 * 1024 * 1024:
                break
            heads //= 2
        plan[j] = (heads, placed)
        placed += batch * (N_HEADS // heads)
    assert placed == n_seq, "sample sequences must fill whole attention calls"
    return plan


def _attention(qt, kb, vt, kmean, slopes, page_table, q_s, k_s, v_s, slope_rows, cache_k, cache_v):
    batch, d, seq = qt.shape
    nq, n_seq, _ = q_s.shape
    blk = MOBA_BLOCK
    n_blocks = seq // blk
    dh = d // N_HEADS
    n_pages = page_table.shape[1]
    pool_pages = cache_k.shape[0]
    ck = cache_k.transpose(0, 2, 3, 1).reshape(pool_pages, d, PAGE_SIZE)
    cv = cache_v.transpose(0, 2, 3, 1).reshape(pool_pages, d, PAGE_SIZE)
    per_seq = lambda a: a.reshape(nq, n_seq, 1, d)
    plan = _attn_plan(n_blocks, batch, d, n_seq, n_pages)
    pt = page_table.reshape(-1)
    out = jnp.zeros((batch, seq, d), BF16)
    out_s = jnp.zeros((nq, n_seq, 1, d), F32)
    for j in range(n_blocks):
        keys = (j + 1) * blk
        heads_per_step, seq_base = plan[j]
        fused = seq_base is not None
        width = heads_per_step * dh
        groups = d // width
        in_specs = [
            pl.BlockSpec(memory_space=pltpu.SMEM),
            pl.BlockSpec((None, width, blk), lambda b, g, pt, j=j: (b, g, j)),
            pl.BlockSpec((None, keys, width), lambda b, g, pt: (b, 0, g)),
            pl.BlockSpec((None, width, keys), lambda b, g, pt: (b, g, 0)),
            pl.BlockSpec((None, n_blocks, width), lambda b, g, pt: (b, 0, g)),
            pl.BlockSpec(memory_space=pl.ANY),
        ]
        args = [pt, slopes, qt, kb, vt, kmean, out]
        out_specs = [pl.BlockSpec((None, blk, width), lambda b, g, pt, j=j: (b, j, g))]
        out_shape = [jax.ShapeDtypeStruct((batch, seq, d), BF16)]
        scratch = []
        aliases = {len(args) - 1: 0}
        if fused:
            seq_spec = pl.BlockSpec((nq, None, 1, d),
                                    lambda b, g, pt, s0=seq_base, groups=groups: (0, s0 + b * groups + g, 0, 0))
            in_specs += [seq_spec, seq_spec, seq_spec,
                         pl.BlockSpec(slope_rows.shape, lambda b, g, pt: (0, 0)),
                         pl.BlockSpec(memory_space=pl.ANY), pl.BlockSpec(memory_space=pl.ANY),
                         pl.BlockSpec(memory_space=pl.ANY)]
            args += [per_seq(q_s), per_seq(k_s), per_seq(v_s), slope_rows, ck, cv, out_s]
            aliases[len(args) - 1] = 1
            out_specs.append(seq_spec)
            out_shape.append(jax.ShapeDtypeStruct((nq, n_seq, 1, d), F32))
            scratch = [pltpu.VMEM((2, n_pages, d, PAGE_SIZE), F32), pltpu.VMEM((2, n_pages, d, PAGE_SIZE), F32),
                       pltpu.SemaphoreType.DMA((2, 2, n_pages))]
        res = pl.pallas_call(
            functools.partial(_attn_kernel, j, fused, seq_base, n_pages),
            grid_spec=pltpu.PrefetchScalarGridSpec(
                num_scalar_prefetch=1, grid=(batch, groups), in_specs=in_specs, out_specs=out_specs,
                scratch_shapes=scratch),
            out_shape=out_shape,
            input_output_aliases=aliases,
            compiler_params=_params(2),
            name=f"attn_{j}",
        )(*args)
        out = res[0]
        if fused:
            out_s = res[1]
    return out, out_s


def kernel(x_prompt, x_sample, c_prompt, c_sample, cache_k, cache_v, page_table, ada_w, ada_b, norm1_g, norm2_g,
           ffn_w_in, ffn_w_out, sgu_w_in, sgu_v_g, sgu_w_s, sgu_b_s, sgu_w_out, kv_ada_w, kv_ada_b, kv_norm_g,
           w_kv, k_norm_g, attn_w_q, q_norm_g, attn_w_o):
    batch, seq, d = x_prompt.shape
    n_seq, nq, _ = x_sample.shape
    dh = d // N_HEADS
    gd = d // N_SGU_GROUPS
    n_tok_p = batch * seq
    n_tok_s = n_seq * nq
    assert seq % TOKEN_TILE == 0 and n_tok_s == TOKEN_TILE and n_seq % SUBLANES == 0 and batch <= SUBLANES

    c_all = jnp.concatenate([c_sample, c_prompt, jnp.zeros((SUBLANES - batch, d), F32)], axis=0)
    mods = _ada(c_all, ada_w, ada_b, 1536)
    kvmods = _ada(c_all, kv_ada_w[None], kv_ada_b[None], 1024)

    bf = lambda w: w.astype(BF16)
    sgu_w_in_b, sgu_w_out_b = bf(sgu_w_in[0]), bf(sgu_w_out[0])
    ffn_in_b, ffn_out_b = bf(ffn_w_in), bf(ffn_w_out)
    w_k_b, w_v_b = bf(w_kv[:, :d]), bf(w_kv[:, d:])
    w_q_b, w_qt_b, w_o_b = bf(attn_w_q[0]), bf(attn_w_q[0].T), bf(attn_w_o[0])
    k_gain = jnp.tile(k_norm_g, N_HEADS).reshape(1, d)
    q_gain = (jnp.tile(q_norm_g[0], N_HEADS) * (dh ** -0.5)).reshape(1, d)
    kv_norm_row = kv_norm_g.reshape(1, d)
    head_of_lane = np.arange(d) // dh
    pool = jnp.asarray((head_of_lane[:, None] == np.arange(LANES)[None, :]) / dh, BF16)
    expand = jnp.asarray(np.arange(LANES)[:, None] == head_of_lane[None, :], BF16)
    slopes = jnp.exp2(-8.0 * jnp.arange(1, N_HEADS + 1, dtype=F32) / N_HEADS)

    wmix_p = sgu_w_s[0]
    bmix_p = jnp.repeat(sgu_b_s[0].T, gd, axis=1)
    wcoef_s = jnp.repeat(sgu_w_s[0][:, :nq, :nq].transpose(1, 2, 0).reshape(nq * nq, N_SGU_GROUPS), gd, axis=1)
    bcoef_s = jnp.repeat(sgu_b_s[0][:, :nq].T, gd, axis=1)

    xp = x_prompt.reshape(n_tok_p, d)
    xs = x_sample.transpose(1, 0, 2).reshape(n_tok_s, d)
    tiles_pb = seq // TOKEN_TILE

    uz_p, sguv_p = _sgu_prompt(xp, batch, mods, n_seq, norm1_g, sgu_w_in_b, sgu_v_g, wmix_p, bmix_p)
    x2_p = _ffn(xp, uz_p, 0, mods, n_seq, tiles_pb, sgu_w_out_b, norm2_g, ffn_in_b, ffn_out_b)
    kt_p, vt_p, kb_p, vtb_p, qt_p, kmean_p = _kvq_prompt(
        x2_p, batch, kvmods, mods, n_seq, kv_norm_row, norm1_g, bf(w_kv[:, :d].T), bf(w_kv[:, d:].T), w_qt_b,
        k_gain.reshape(d, 1), (q_gain * LOG2_E).reshape(d, 1))
    uz_s, sguv_s = _sgu_sample(xs, nq, mods, n_seq, norm1_g, sgu_w_in_b, sgu_v_g, wcoef_s, bcoef_s)
    x2_s = _ffn(xs, uz_s, 0, mods, n_seq, 0, sgu_w_out_b, norm2_g, ffn_in_b, ffn_out_b)
    k_s, v_s, q_s, kt_s, vt_s = _kvq_sample(x2_s, nq, kvmods, mods, n_seq, kv_norm_row, norm1_g, w_k_b, w_v_b,
                                            w_q_b, k_gain, q_gain, pool, expand)

    slope_rows = jnp.tile(slopes, nq).reshape(nq * N_HEADS, 1)
    o_p, o_s = _attention(qt_p, kb_p.reshape(batch, seq, d), vtb_p, kmean_p.reshape(batch, seq // MOBA_BLOCK, d),
                          slopes * LOG2_E, page_table, q_s.reshape(nq, n_seq, d), k_s.reshape(nq, n_seq, d),
                          v_s.reshape(nq, n_seq, d), slope_rows, cache_k, cache_v)
    y_p = _ffn(x2_p, o_p.reshape(n_tok_p, d), 1, mods, n_seq, tiles_pb, w_o_b, norm2_g, ffn_in_b, ffn_out_b)
    y_s = _ffn(x2_s, o_s.reshape(n_tok_s, d), 1, mods, n_seq, 0, w_o_b, norm2_g, ffn_in_b, ffn_out_b)

    by_seq = lambda a: a.reshape(nq, n_seq, d).transpose(1, 0, 2)
    return (y_p.reshape(batch, seq, d), by_seq(y_s),
            kt_p.reshape(batch, N_HEADS, dh, seq).transpose(0, 3, 1, 2),
            vt_p.reshape(batch, N_HEADS, dh, seq).transpose(0, 3, 1, 2),
            kt_s.reshape(nq, N_HEADS, dh, n_seq).transpose(3, 0, 1, 2),
            vt_s.reshape(nq, N_HEADS, dh, n_seq).transpose(3, 0, 1, 2),
            sguv_p.reshape(1, batch, CHUNK, d), by_seq(sguv_s)[None])
```

```python
import functools

import numpy as np
import jax
import jax.numpy as jnp
from jax import lax
from jax.experimental import pallas as pl
from jax.experimental.pallas import tpu as pltpu

F32 = jnp.float32
BF16 = jnp.bfloat16
EPS = 1e-6
NEG_INF = float("-inf")
LOG2_E = 1.4426950408889634

V7X_VMEM_LIMIT_BYTES = 56 * 1024 * 1024
LANES = 128
SUBLANES = 8

N_HEADS = 16
MOBA_BLOCK = 256
MOBA_TOP_K = 3
CHUNK = 128
N_SGU_GROUPS = 8
PAGE_SIZE = 128
TOKEN_TILE = 512
ATTN_HEADS_PER_STEP = 8
ATTN_HEADS_PER_FUSED_STEP = 4


def _params(n_grid_dims):
    return pltpu.CompilerParams(
        dimension_semantics=("arbitrary",) * n_grid_dims,
        vmem_limit_bytes=V7X_VMEM_LIMIT_BYTES)


def _const_spec(shape):
    n = len(shape)
    return pl.BlockSpec(shape, lambda *_: (0,) * n, pipeline_mode=pl.Buffered(1))


def _layer_spec(w, layer):
    tail = (0,) * (w.ndim - 1)
    return pl.BlockSpec((None,) + w.shape[1:], lambda *_: (layer,) + tail, pipeline_mode=pl.Buffered(1))


def _mod_spec(mods, layer, n_seq, tiles_per_batch):
    w = mods.shape[-1]
    if tiles_per_batch:
        return pl.BlockSpec((None, SUBLANES, w), lambda *_: (layer, n_seq // SUBLANES, 0))
    return pl.BlockSpec((None, n_seq, w), lambda *_: (layer, 0, 0))


def _mod_rows(mod_ref, tiles_per_batch):
    if tiles_per_batch:
        return mod_ref[pl.ds(pl.program_id(0) // tiles_per_batch, 1), :]
    return mod_ref[...]


def _per_token(v, tm):
    if v.shape[0] in (1, tm):
        return v
    return jnp.concatenate([v] * (tm // v.shape[0]), axis=0)


def _dot(a, b):
    return jnp.dot(a, b, preferred_element_type=F32)


def _dot_nt(a, b):
    return lax.dot_general(a, b, (((1,), (1,)), ((), ())), preferred_element_type=F32)


def _split_bf16(x):
    hi = x.astype(BF16)
    lo = (x - hi.astype(F32)).astype(BF16)
    return hi, lo


def _rms(x, g):
    return x * lax.rsqrt(jnp.mean(x * x, axis=-1, keepdims=True) + EPS) * g


def _modulate(x, g, shift, scale):
    tm = x.shape[0]
    return _rms(x, g) * (1.0 + _per_token(scale, tm)) + _per_token(shift, tm)


def _head_rms(x, gain, pool_ref, expand_ref):
    sq_hi, sq_lo = _split_bf16(x * x)
    pool = pool_ref[...]
    ms = _dot(sq_hi, pool) + _dot(sq_lo, pool)
    r_hi, r_lo = _split_bf16(lax.rsqrt(ms + EPS))
    expand = expand_ref[...]
    return x * (_dot(r_hi, expand) + _dot(r_lo, expand)) * gain


def _head_rms_t(xt, gain_col):
    d, tm = xt.shape
    x3 = xt.reshape(N_HEADS, d // N_HEADS, tm)
    x3 = x3 * lax.rsqrt(jnp.mean(x3 * x3, axis=1, keepdims=True) + EPS)
    return x3.reshape(d, tm) * gain_col


def _ada_kernel(c_ref, w_ref, b_ref, o_ref):
    c = c_ref[...]
    s = (c * jax.nn.sigmoid(c)).astype(BF16)
    o_ref[...] = _dot(s, w_ref[...].astype(BF16)) + b_ref[pl.ds(pl.program_id(0), 1), :]


def _ada(c_all, w, b, tn):
    n_layers, d, n = w.shape
    m = c_all.shape[0]
    return pl.pallas_call(
        _ada_kernel,
        grid=(n_layers, n // tn),
        in_specs=[
            pl.BlockSpec((m, d), lambda l, j: (0, 0)),
            pl.BlockSpec((None, d, tn), lambda l, j: (l, 0, j)),
            pl.BlockSpec((n_layers, tn), lambda l, j: (0, j)),
        ],
        out_specs=pl.BlockSpec((None, m, tn), lambda l, j: (l, 0, j)),
        out_shape=jax.ShapeDtypeStruct((n_layers, m, n), F32),
        compiler_params=_params(2),
        name="ada",
    )(c_all, w, b)


def _sgu_front(x_ref, mod, g_ref, win_ref, vg_ref):
    d = x_ref.shape[1]
    h = _modulate(x_ref[...], g_ref[0:1, :], mod[:, 0:d], mod[:, d:2 * d]).astype(BF16)
    uv = jax.nn.gelu(_dot(h, win_ref[...]))
    return uv[:, :d], _rms(uv[:, d:], vg_ref[...])


def _sgu_prompt_kernel(tiles_per_batch, x_ref, mod_ref, g_ref, win_ref, vg_ref, wmix_ref, bmix_ref, uz_ref, v_ref):
    tm, d = x_ref.shape
    c_rows = wmix_ref.shape[1]
    gd = d // N_SGU_GROUPS
    u, v = _sgu_front(x_ref, _mod_rows(mod_ref, tiles_per_batch), g_ref, win_ref, vg_ref)
    v_ref[...] = v[tm - v_ref.shape[0]:, :]
    vb = v.astype(BF16)
    causal = (lax.broadcasted_iota(jnp.int32, (c_rows, c_rows), 1)
              <= lax.broadcasted_iota(jnp.int32, (c_rows, c_rows), 0))
    for g in range(N_SGU_GROUPS):
        wg = jnp.where(causal, wmix_ref[g], 0.0).astype(BF16)
        cols = slice(g * gd, (g + 1) * gd)
        for ch in range(tm // c_rows):
            rows = slice(ch * c_rows, (ch + 1) * c_rows)
            mixed = _dot(wg, vb[rows, cols]) + bmix_ref[:, cols]
            uz_ref[rows, cols] = (u[rows, cols] * mixed).astype(BF16)


def _sgu_sample_kernel(nq, x_ref, mod_ref, g_ref, win_ref, vg_ref, wcoef_ref, bcoef_ref, uz_ref, v_ref):
    tm = x_ref.shape[0]
    ns = tm // nq
    u, v = _sgu_front(x_ref, mod_ref[...], g_ref, win_ref, vg_ref)
    v_ref[...] = v
    for t in range(nq):
        mixed = bcoef_ref[t:t + 1, :]
        for s in range(t + 1):
            mixed = mixed + wcoef_ref[t * nq + s:t * nq + s + 1, :] * v[s * ns:(s + 1) * ns, :]
        uz_ref[t * ns:(t + 1) * ns, :] = (u[t * ns:(t + 1) * ns, :] * mixed).astype(BF16)


def _sgu_sample(x, nq, mods, n_seq, norm_g, w_in, v_g, wcoef, bcoef):
    t, d = x.shape
    full = pl.BlockSpec((t, d), lambda i: (0, 0))
    return pl.pallas_call(
        functools.partial(_sgu_sample_kernel, nq),
        grid=(1,),
        in_specs=[
            full,
            _mod_spec(mods, 0, n_seq, 0),
            _const_spec(norm_g.shape),
            _const_spec(w_in.shape),
            _const_spec(v_g.shape),
            _const_spec(wcoef.shape),
            _const_spec(bcoef.shape),
        ],
        out_specs=[full, full],
        out_shape=[jax.ShapeDtypeStruct((t, d), BF16), jax.ShapeDtypeStruct((t, d), F32)],
        compiler_params=_params(1),
        name="sgu_sample",
    )(x, mods, norm_g, w_in, v_g, wcoef, bcoef)


def _ff_chunks(d_ff):
    step = 768
    return tuple((s, min(step, d_ff - s)) for s in range(0, d_ff, step))


def _ffn_kernel(layer, tiles_per_batch, x_ref, z_ref, mod_ref, wp_ref, g_ref, win_ref, wout_ref, o_ref):
    tm, d = x_ref.shape
    d_ff = wout_ref.shape[0]
    mod = _mod_rows(mod_ref, tiles_per_batch)
    gate1 = _per_token(mod[:, 2 * d:3 * d], tm)
    gate2 = _per_token(mod[:, 5 * d:6 * d], tm)
    x1 = x_ref[...] + gate1 * _dot(z_ref[...].astype(BF16), wp_ref[...])
    h = _modulate(x1, g_ref[layer:layer + 1, :], mod[:, 3 * d:4 * d], mod[:, 4 * d:5 * d]).astype(BF16)
    acc = None
    for s, w in _ff_chunks(d_ff):
        a = _dot(h, win_ref[:, s:s + w])
        b = _dot(h, win_ref[:, d_ff + s:d_ff + s + w])
        act = (a * jax.nn.sigmoid(a) * b).astype(BF16)
        part = _dot(act, wout_ref[s:s + w, :])
        acc = part if acc is None else acc + part
    o_ref[...] = x1 + gate2 * acc


def _ffn(x, z, layer, mods, n_seq, tiles_per_batch, w_proj, norm_g, w_in, w_out):
    t, d = x.shape
    tm = TOKEN_TILE
    return pl.pallas_call(
        functools.partial(_ffn_kernel, layer, tiles_per_batch),
        grid=(t // tm,),
        in_specs=[
            pl.BlockSpec((tm, d), lambda i: (i, 0)),
            pl.BlockSpec((tm, d), lambda i: (i, 0)),
            _mod_spec(mods, layer, n_seq, tiles_per_batch),
            _const_spec(w_proj.shape),
            _const_spec(norm_g.shape),
            _layer_spec(w_in, layer),
            _layer_spec(w_out, layer),
        ],
        out_specs=pl.BlockSpec((tm, d), lambda i: (i, 0)),
        out_shape=jax.ShapeDtypeStruct((t, d), F32),
        compiler_params=_params(1),
        name="ffn",
    )(x, z, mods, w_proj, norm_g, w_in, w_out)


def _layer0_prompt_kernel(tiles_per_batch, x_ref, mod_ref, g1_ref, swin_ref, vg_ref, wmix_ref, bmix_ref, wp_ref,
                          g2_ref, win_ref, wout_ref, o_ref, v_ref, uz_ref):
    _sgu_prompt_kernel(tiles_per_batch, x_ref, mod_ref, g1_ref, swin_ref, vg_ref, wmix_ref, bmix_ref, uz_ref, v_ref)
    _ffn_kernel(0, tiles_per_batch, x_ref, uz_ref, mod_ref, wp_ref, g2_ref, win_ref, wout_ref, o_ref)


def _layer0_prompt(x, batch, mods, n_seq, norm1_g, sgu_w_in, v_g, wmix, bmix, w_proj, norm2_g, w_in, w_out):
    t, d = x.shape
    tm = TOKEN_TILE
    tiles_per_batch = t // batch // tm
    return pl.pallas_call(
        functools.partial(_layer0_prompt_kernel, tiles_per_batch),
        grid=(t // tm,),
        in_specs=[
            pl.BlockSpec((tm, d), lambda i: (i, 0)),
            _mod_spec(mods, 0, n_seq, tiles_per_batch),
            _const_spec(norm1_g.shape),
            _const_spec(sgu_w_in.shape),
            _const_spec(v_g.shape),
            _const_spec(wmix.shape),
            _const_spec(bmix.shape),
            _const_spec(w_proj.shape),
            _const_spec(norm2_g.shape),
            _layer_spec(w_in, 0),
            _layer_spec(w_out, 0),
        ],
        out_specs=[pl.BlockSpec((tm, d), lambda i: (i, 0)),
                   pl.BlockSpec((None, CHUNK, d), lambda i: (i // tiles_per_batch, 0, 0))],
        out_shape=[jax.ShapeDtypeStruct((t, d), F32), jax.ShapeDtypeStruct((batch, CHUNK, d), F32)],
        scratch_shapes=[pltpu.VMEM((tm, d), BF16)],
        compiler_params=_params(1),
        name="layer0_prompt",
    )(x, mods, norm1_g, sgu_w_in, v_g, wmix, bmix, w_proj, norm2_g, w_in, w_out)


def _kvq_prompt_kernel(tiles_per_batch, x_ref, kvmod_ref, mod_ref, kvg_ref, n1g_ref, wkt_ref, wvt_ref, wqt_ref,
                       kg_ref, qg_ref, kt_ref, vt_ref, kb_ref, vtb_ref, qt_ref, kmean_ref):
    tm, d = x_ref.shape
    x = x_ref[...]
    kvmod = _mod_rows(kvmod_ref, tiles_per_batch)
    mod = _mod_rows(mod_ref, tiles_per_batch)
    hkv = _modulate(x, kvg_ref[...], kvmod[:, 0:d], kvmod[:, d:2 * d]).astype(BF16)
    kt = _head_rms_t(_dot_nt(wkt_ref[...], hkv), kg_ref[...])
    vt = _dot_nt(wvt_ref[...], hkv)
    kt_ref[...] = kt
    vt_ref[...] = vt
    vtb_ref[...] = vt.astype(BF16)
    hq = _modulate(x, n1g_ref[1:2, :], mod[:, 0:d], mod[:, d:2 * d]).astype(BF16)
    qt_ref[...] = _head_rms_t(_dot_nt(wqt_ref[...], hq), qg_ref[...]).astype(BF16)
    for i in range(tm // MOBA_BLOCK):
        cols = slice(i * MOBA_BLOCK, (i + 1) * MOBA_BLOCK)
        k_nat = kt[:, cols].T
        kmean_ref[i] = jnp.mean(k_nat, axis=0, keepdims=True)
        kb_ref[cols, :] = k_nat.astype(BF16)


def _kvq_prompt(x, batch, kvmods, mods, n_seq, kv_norm_g, norm1_g, w_kt, w_vt, w_qt, k_gain_col, q_gain_col):
    t, d = x.shape
    tm = TOKEN_TILE
    seq = t // batch
    tiles_per_batch = seq // tm
    nblk = tm // MOBA_BLOCK
    row_spec = pl.BlockSpec((tm, d), lambda i: (i, 0))
    col_spec = pl.BlockSpec((None, d, tm), lambda i: (i // tiles_per_batch, 0, i % tiles_per_batch))
    return pl.pallas_call(
        functools.partial(_kvq_prompt_kernel, tiles_per_batch),
        grid=(t // tm,),
        in_specs=[
            row_spec,
            _mod_spec(kvmods, 0, n_seq, tiles_per_batch),
            _mod_spec(mods, 1, n_seq, tiles_per_batch),
            _const_spec(kv_norm_g.shape), _const_spec(norm1_g.shape),
            _const_spec(w_kt.shape), _const_spec(w_vt.shape), _const_spec(w_qt.shape),
            _const_spec((d, 1)), _const_spec((d, 1)),
        ],
        out_specs=[
            col_spec, col_spec, row_spec, col_spec, col_spec,
            pl.BlockSpec((nblk, 1, d), lambda i: (i, 0, 0)),
        ],
        out_shape=[
            jax.ShapeDtypeStruct((batch, d, seq), F32),
            jax.ShapeDtypeStruct((batch, d, seq), F32),
            jax.ShapeDtypeStruct((t, d), BF16),
            jax.ShapeDtypeStruct((batch, d, seq), BF16),
            jax.ShapeDtypeStruct((batch, d, seq), BF16),
            jax.ShapeDtypeStruct((t // MOBA_BLOCK, 1, d), F32),
        ],
        compiler_params=_params(1),
        name="kvq_prompt",
    )(x, kvmods, mods, kv_norm_g, norm1_g, w_kt, w_vt, w_qt, k_gain_col, q_gain_col)


def _kvq_sample_kernel(nq, x_ref, kvmod_ref, mod_ref, kvg_ref, n1g_ref, wk_ref, wv_ref, wq_ref, kg_ref, qg_ref,
                       pool_ref, expand_ref, k_ref, v_ref, q_ref, kt_ref, vt_ref):
    tm, d = x_ref.shape
    ns = tm // nq
    x = x_ref[...]
    kvmod = kvmod_ref[...]
    mod = mod_ref[...]
    hkv = _modulate(x, kvg_ref[...], kvmod[:, 0:d], kvmod[:, d:2 * d]).astype(BF16)
    k = _head_rms(_dot(hkv, wk_ref[...]), kg_ref[...], pool_ref, expand_ref)
    v = _dot(hkv, wv_ref[...])
    k_ref[...] = k
    v_ref[...] = v
    for t in range(nq):
        kt_ref[t] = k[t * ns:(t + 1) * ns, :].T
        vt_ref[t] = v[t * ns:(t + 1) * ns, :].T
    hq = _modulate(x, n1g_ref[1:2, :], mod[:, 0:d], mod[:, d:2 * d]).astype(BF16)
    q_ref[...] = _head_rms(_dot(hq, wq_ref[...]), qg_ref[...], pool_ref, expand_ref)


def _kvq_sample(x, nq, kvmods, mods, n_seq, kv_norm_g, norm1_g, w_k, w_v, w_q, k_gain, q_gain, pool, expand):
    t, d = x.shape
    full = pl.BlockSpec((t, d), lambda i: (0, 0))
    tspec = pl.BlockSpec((nq, d, t // nq), lambda i: (0, 0, 0))
    return pl.pallas_call(
        functools.partial(_kvq_sample_kernel, nq),
        grid=(1,),
        in_specs=[
            full,
            _mod_spec(kvmods, 0, n_seq, 0),
            _mod_spec(mods, 1, n_seq, 0),
            _const_spec(kv_norm_g.shape), _const_spec(norm1_g.shape),
            _const_spec(w_k.shape), _const_spec(w_v.shape), _const_spec(w_q.shape),
            _const_spec((1, d)), _const_spec((1, d)),
            _const_spec(pool.shape), _const_spec(expand.shape),
        ],
        out_specs=[full, full, full, tspec, tspec],
        out_shape=[jax.ShapeDtypeStruct((t, d), F32)] * 3 + [jax.ShapeDtypeStruct((nq, d, t // nq), F32)] * 2,
        compiler_params=_params(1),
        name="kvq_sample",
    )(x, kvmods, mods, kv_norm_g, norm1_g, w_k, w_v, w_q, k_gain, q_gain, pool, expand)


def _top_k_additive_mask(gate, own, axis):
    nb = gate.shape[axis]
    idx = lax.broadcasted_iota(jnp.int32, gate.shape, axis)
    eligible = idx < own
    g = jnp.where(eligible, gate, NEG_INF)
    mask = jnp.full(gate.shape, NEG_INF, F32)
    for _ in range(min(MOBA_TOP_K, nb)):
        best = jnp.max(g, axis=axis, keepdims=True)
        first = jnp.min(jnp.where(g == best, idx, nb), axis=axis, keepdims=True)
        pick = idx == first
        mask = jnp.where(pick, 0.0, mask)
        g = jnp.where(pick, NEG_INF, g)
    return jnp.where(eligible, mask, NEG_INF)


def _prompt_attention(j, slopes_ref, qt_ref, kb_ref, vt_ref, kmean_ref, o_ref, hooks):
    group = pl.program_id(1)
    blk = MOBA_BLOCK
    pair_w = LANES
    n_heads = 2 * (qt_ref.shape[0] // pair_w)
    dh = pair_w // 2
    zeros = jnp.zeros((dh, blk), BF16)
    causal = (lax.broadcasted_iota(jnp.int32, (blk, blk), 0)
              <= lax.broadcasted_iota(jnp.int32, (blk, blk), 1))
    n_feat = 3
    k_feat = jnp.where(lax.broadcasted_iota(jnp.int32, (blk, LANES), 1) < n_feat,
                       lax.broadcasted_iota(jnp.int32, (blk, LANES), 0), 0).astype(F32).astype(BF16)
    feat_row = lax.broadcasted_iota(jnp.int32, (LANES, blk), 0)
    ones_rows = jnp.ones((16, blk), BF16)

    def prepare(h):
        lanes = slice((h // 2) * pair_w, (h // 2 + 1) * pair_w)
        slope = slopes_ref[group * n_heads + h]
        qt = qt_ref[lanes, :]
        qh = jnp.concatenate([qt[:dh], zeros] if h % 2 == 0 else [zeros, qt[dh:]], axis=0)
        q_feat = jnp.zeros((LANES, blk), F32)
        rest_slope = jnp.full((1, blk), slope, F32)
        for f in range(n_feat):
            piece = rest_slope.astype(BF16).astype(F32)
            q_feat = jnp.where(feat_row == f, piece, q_feat)
            rest_slope = rest_slope - piece
        q_aug = jnp.concatenate([qh, q_feat.astype(BF16)], axis=0)
        rows = []
        if j > 0:
            km_hi, km_lo = _split_bf16(kmean_ref[:, lanes])
            sel = _top_k_additive_mask(_dot(km_hi, qh) + _dot(km_lo, qh), j, 0)
            rows = [sel[i:i + 1, :] + slope * float((i - j) * blk) for i in range(j)]
        rows.append(jnp.zeros((1, blk), F32))
        return q_aug, rows

    def scores(h, q_aug, i):
        lanes = slice((h // 2) * pair_w, (h // 2 + 1) * pair_w)
        t = _dot(jnp.concatenate([kb_ref[i * blk:(i + 1) * blk, lanes], k_feat], axis=1), q_aug)
        return jnp.where(causal, t, NEG_INF) if i == j else t

    def values(h, t, shift, i):
        p = jnp.exp2((t - shift).astype(BF16))
        vt_aug = jnp.concatenate([vt_ref[h * dh:(h + 1) * dh, i * blk:(i + 1) * blk], ones_rows], axis=0)
        return _dot(vt_aug, p)

    outs = []
    prev = None
    for h in range(n_heads + 1):
        if h in hooks:
            hooks[h]()
        if h < n_heads:
            q_aug, rows = prepare(h)
            ts, m = [], None
        acc = None
        for i in range(j + 1):
            if h < n_heads:
                t = scores(h, q_aug, i)
                ts.append(t)
                mi = jnp.max(t, axis=0, keepdims=True) + rows[i]
                m = mi if m is None else jnp.maximum(m, mi)
            if prev is not None:
                p_ts, p_m, p_rows = prev
                part = values(h - 1, p_ts[i], p_m - p_rows[i], i)
                acc = part if acc is None else acc + part
        if prev is not None:
            outs.append(acc[:dh] / acc[dh:dh + 1])
        prev = (ts, m, rows) if h < n_heads else None
    o_ref[...] = jnp.concatenate(outs, axis=0).T.astype(o_ref.dtype)


def _sample_attention(n_pages, q_ref, kn_ref, vn_ref, slope_ref, kbuf, vbuf, o_ref):
    nq, _, d = q_ref.shape
    dh = d // N_HEADS
    rows = nq * N_HEADS
    pages_per_block = MOBA_BLOCK // PAGE_SIZE
    n_past = n_pages // pages_per_block
    past_len = n_pages * PAGE_SIZE

    lane_head = lax.broadcasted_iota(jnp.int32, (N_HEADS, d), 1) // dh
    head_mask = lane_head == lax.broadcasted_iota(jnp.int32, (N_HEADS, d), 0)
    qrows = jnp.concatenate([jnp.where(head_mask, q_ref[i], 0.0) for i in range(nq)], axis=0)
    qrows_b = qrows.astype(BF16)
    row_head_mask = jnp.concatenate([head_mask] * nq, axis=0)
    slope = slope_ref[...]
    row_q = lax.broadcasted_iota(jnp.int32, (rows, 1), 0) // N_HEADS

    scores = [_dot(qrows_b, kbuf[p].astype(BF16)) for p in range(n_pages)]
    gate = jnp.concatenate(
        [sum(jnp.sum(s, axis=1, keepdims=True) for s in scores[n * pages_per_block:(n + 1) * pages_per_block])
         for n in range(n_past)], axis=1) * (1.0 / MOBA_BLOCK)
    sel = _top_k_additive_mask(gate, n_past, 1)

    dist0 = (past_len + row_q).astype(F32)
    offs = lax.broadcasted_iota(jnp.int32, (1, PAGE_SIZE), 1)
    masked = []
    for p in range(n_pages):
        n = p // pages_per_block
        kpos = (p * PAGE_SIZE + offs).astype(F32)
        masked.append(scores[p] - slope * (dist0 - kpos) + sel[:, n:n + 1])
    own = []
    for t in range(nq):
        s = jnp.sum(qrows * kn_ref[t], axis=1, keepdims=True)
        s = s - slope * (row_q - t).astype(F32)
        own.append(jnp.where(row_q >= t, s, NEG_INF))
    m = own[0]
    for s in own[1:]:
        m = jnp.maximum(m, s)
    for s in masked:
        m = jnp.maximum(m, jnp.max(s, axis=1, keepdims=True))
    l = jnp.zeros((rows, 1), F32)
    acc = jnp.zeros((rows, d), F32)
    for t in range(nq):
        p_own = jnp.exp(own[t] - m)
        l = l + p_own
        acc = acc + p_own * vn_ref[t]
    for p in range(n_pages):
        pr = jnp.exp(masked[p] - m)
        l = l + jnp.sum(pr, axis=1, keepdims=True)
        acc = acc + _dot_nt(pr.astype(BF16), vbuf[p].astype(BF16))
    o_rows = jnp.where(row_head_mask, acc / l, 0.0)
    o = jnp.sum(o_rows.reshape(nq, N_HEADS, d), axis=1)
    for t in range(nq):
        o_ref[t] = o[t:t + 1, :]


def _attn_kernel(j, fused, seq_base, n_pages, pt_ref, slopes_ref, qt_ref, kb_ref, vt_ref, kmean_ref,
                 o_in_ref, *rest):
    if not fused:
        (o_ref,) = rest
        _prompt_attention(j, slopes_ref, qt_ref, kb_ref, vt_ref, kmean_ref, o_ref, {})
        return
    qs_ref, kn_ref, vn_ref, srow_ref, ck_ref, cv_ref, os_in_ref, o_ref, os_ref, kbuf, vbuf, sem = rest
    n_steps = pl.num_programs(0) * pl.num_programs(1)
    step = pl.program_id(0) * pl.num_programs(1) + pl.program_id(1)
    slot = step % 2

    def page_copies(at_step, at_slot):
        seq = seq_base + at_step
        copies = []
        for i in range(n_pages):
            page = pt_ref[seq * n_pages + i]
            copies.append(pltpu.make_async_copy(ck_ref.at[page], kbuf.at[at_slot, i], sem.at[at_slot, 0, i]))
            copies.append(pltpu.make_async_copy(cv_ref.at[page], vbuf.at[at_slot, i], sem.at[at_slot, 1, i]))
        return copies

    @pl.when(step == 0)
    def _():
        for c in page_copies(step, slot):
            c.start()

    @pl.when(step < n_steps)
    def _():
        for c in page_copies(step, slot):
            c.wait()

    @pl.when(step + 1 < n_steps)
    def _():
        for c in page_copies(step + 1, 1 - slot):
            c.start()

    _prompt_attention(j, slopes_ref, qt_ref, kb_ref, vt_ref, kmean_ref, o_ref, {})
    _sample_attention(n_pages, qs_ref, kn_ref, vn_ref, srow_ref, kbuf.at[slot], vbuf.at[slot], os_ref)


def _attention(qt, kb, vt, kmean, slopes, page_table, q_s, k_s, v_s, slope_rows, cache_k, cache_v):
    batch, d, seq = qt.shape
    nq, n_seq, _ = q_s.shape
    blk = MOBA_BLOCK
    n_blocks = seq // blk
    dh = d // N_HEADS
    n_pages = page_table.shape[1]
    pool_pages = cache_k.shape[0]
    ck = cache_k.transpose(0, 2, 3, 1).reshape(pool_pages, d, PAGE_SIZE)
    cv = cache_v.transpose(0, 2, 3, 1).reshape(pool_pages, d, PAGE_SIZE)
    per_seq = lambda a: a.reshape(nq, n_seq, 1, d)
    fused_steps = batch * (N_HEADS // ATTN_HEADS_PER_FUSED_STEP)
    assert n_seq % fused_steps == 0 and n_seq // fused_steps <= n_blocks
    first_fused = n_blocks - n_seq // fused_steps
    pt = page_table.reshape(-1)
    out = jnp.zeros((batch, seq, d), BF16)
    out_s = jnp.zeros((nq, n_seq, 1, d), F32)
    for j in range(n_blocks):
        keys = (j + 1) * blk
        fused = j >= first_fused
        width = (ATTN_HEADS_PER_FUSED_STEP if fused else ATTN_HEADS_PER_STEP) * dh
        groups = d // width
        seq_base = (j - first_fused) * fused_steps
        in_specs = [
            pl.BlockSpec(memory_space=pltpu.SMEM),
            pl.BlockSpec((None, width, blk), lambda b, g, pt, j=j: (b, g, j)),
            pl.BlockSpec((None, keys, width), lambda b, g, pt: (b, 0, g)),
            pl.BlockSpec((None, width, keys), lambda b, g, pt: (b, g, 0)),
            pl.BlockSpec((None, n_blocks, width), lambda b, g, pt: (b, 0, g)),
            pl.BlockSpec(memory_space=pl.ANY),
        ]
        args = [pt, slopes, qt, kb, vt, kmean, out]
        out_specs = [pl.BlockSpec((None, blk, width), lambda b, g, pt, j=j: (b, j, g))]
        out_shape = [jax.ShapeDtypeStruct((batch, seq, d), BF16)]
        scratch = []
        aliases = {len(args) - 1: 0}
        if fused:
            seq_spec = pl.BlockSpec((nq, None, 1, d),
                                    lambda b, g, pt, s0=seq_base, groups=groups: (0, s0 + b * groups + g, 0, 0))
            in_specs += [seq_spec, seq_spec, seq_spec,
                         pl.BlockSpec(slope_rows.shape, lambda b, g, pt: (0, 0)),
                         pl.BlockSpec(memory_space=pl.ANY), pl.BlockSpec(memory_space=pl.ANY),
                         pl.BlockSpec(memory_space=pl.ANY)]
            args += [per_seq(q_s), per_seq(k_s), per_seq(v_s), slope_rows, ck, cv, out_s]
            aliases[len(args) - 1] = 1
            out_specs.append(seq_spec)
            out_shape.append(jax.ShapeDtypeStruct((nq, n_seq, 1, d), F32))
            scratch = [pltpu.VMEM((2, n_pages, d, PAGE_SIZE), F32), pltpu.VMEM((2, n_pages, d, PAGE_SIZE), F32),
                       pltpu.SemaphoreType.DMA((2, 2, n_pages))]
        res = pl.pallas_call(
            functools.partial(_attn_kernel, j, fused, seq_base, n_pages),
            grid_spec=pltpu.PrefetchScalarGridSpec(
                num_scalar_prefetch=1, grid=(batch, groups), in_specs=in_specs, out_specs=out_specs,
                scratch_shapes=scratch),
            out_shape=out_shape,
            input_output_aliases=aliases,
            compiler_params=_params(2),
            name=f"attn_{j}",
        )(*args)
        out = res[0]
        if fused:
            out_s = res[1]
    return out, out_s


def kernel(x_prompt, x_sample, c_prompt, c_sample, cache_k, cache_v, page_table, ada_w, ada_b, norm1_g, norm2_g,
           ffn_w_in, ffn_w_out, sgu_w_in, sgu_v_g, sgu_w_s, sgu_b_s, sgu_w_out, kv_ada_w, kv_ada_b, kv_norm_g,
           w_kv, k_norm_g, attn_w_q, q_norm_g, attn_w_o):
    batch, seq, d = x_prompt.shape
    n_seq, nq, _ = x_sample.shape
    dh = d // N_HEADS
    gd = d // N_SGU_GROUPS
    n_tok_p = batch * seq
    n_tok_s = n_seq * nq
    assert seq % TOKEN_TILE == 0 and n_tok_s == TOKEN_TILE and n_seq % SUBLANES == 0 and batch <= SUBLANES

    c_all = jnp.concatenate([c_sample, c_prompt, jnp.zeros((SUBLANES - batch, d), F32)], axis=0)
    mods = _ada(c_all, ada_w, ada_b, 1536)
    kvmods = _ada(c_all, kv_ada_w[None], kv_ada_b[None], 1024)

    bf = lambda w: w.astype(BF16)
    sgu_w_in_b, sgu_w_out_b = bf(sgu_w_in[0]), bf(sgu_w_out[0])
    ffn_in_b, ffn_out_b = bf(ffn_w_in), bf(ffn_w_out)
    w_k_b, w_v_b = bf(w_kv[:, :d]), bf(w_kv[:, d:])
    w_q_b, w_qt_b, w_o_b = bf(attn_w_q[0]), bf(attn_w_q[0].T), bf(attn_w_o[0])
    k_gain = jnp.tile(k_norm_g, N_HEADS).reshape(1, d)
    q_gain = (jnp.tile(q_norm_g[0], N_HEADS) * (dh ** -0.5)).reshape(1, d)
    kv_norm_row = kv_norm_g.reshape(1, d)
    head_of_lane = np.arange(d) // dh
    pool = jnp.asarray((head_of_lane[:, None] == np.arange(LANES)[None, :]) / dh, BF16)
    expand = jnp.asarray(np.arange(LANES)[:, None] == head_of_lane[None, :], BF16)
    slopes = jnp.exp2(-8.0 * jnp.arange(1, N_HEADS + 1, dtype=F32) / N_HEADS)

    wmix_p = sgu_w_s[0]
    bmix_p = jnp.repeat(sgu_b_s[0].T, gd, axis=1)
    wcoef_s = jnp.repeat(sgu_w_s[0][:, :nq, :nq].transpose(1, 2, 0).reshape(nq * nq, N_SGU_GROUPS), gd, axis=1)
    bcoef_s = jnp.repeat(sgu_b_s[0][:, :nq].T, gd, axis=1)

    xp = x_prompt.reshape(n_tok_p, d)
    xs = x_sample.transpose(1, 0, 2).reshape(n_tok_s, d)
    tiles_pb = seq // TOKEN_TILE

    x2_p, sguv_p = _layer0_prompt(xp, batch, mods, n_seq, norm1_g, sgu_w_in_b, sgu_v_g, wmix_p, bmix_p,
                                  sgu_w_out_b, norm2_g, ffn_in_b, ffn_out_b)
    kt_p, vt_p, kb_p, vtb_p, qt_p, kmean_p = _kvq_prompt(
        x2_p, batch, kvmods, mods, n_seq, kv_norm_row, norm1_g, bf(w_kv[:, :d].T), bf(w_kv[:, d:].T), w_qt_b,
        k_gain.reshape(d, 1), (q_gain * LOG2_E).reshape(d, 1))
    uz_s, sguv_s = _sgu_sample(xs, nq, mods, n_seq, norm1_g, sgu_w_in_b, sgu_v_g, wcoef_s, bcoef_s)
    x2_s = _ffn(xs, uz_s, 0, mods, n_seq, 0, sgu_w_out_b, norm2_g, ffn_in_b, ffn_out_b)
    k_s, v_s, q_s, kt_s, vt_s = _kvq_sample(x2_s, nq, kvmods, mods, n_seq, kv_norm_row, norm1_g, w_k_b, w_v_b,
                                            w_q_b, k_gain, q_gain, pool, expand)

    slope_rows = jnp.tile(slopes, nq).reshape(nq * N_HEADS, 1)
    o_p, o_s = _attention(qt_p, kb_p.reshape(batch, seq, d), vtb_p, kmean_p.reshape(batch, seq // MOBA_BLOCK, d),
                          slopes * LOG2_E, page_table, q_s.reshape(nq, n_seq, d), k_s.reshape(nq, n_seq, d),
                          v_s.reshape(nq, n_seq, d), slope_rows, cache_k, cache_v)
    y_p = _ffn(x2_p, o_p.reshape(n_tok_p, d), 1, mods, n_seq, tiles_pb, w_o_b, norm2_g, ffn_in_b, ffn_out_b)
    y_s = _ffn(x2_s, o_s.reshape(n_tok_s, d), 1, mods, n_seq, 0, w_o_b, norm2_g, ffn_in_b, ffn_out_b)

    by_seq = lambda a: a.reshape(nq, n_seq, d).transpose(1, 0, 2)
    return (y_p.reshape(batch, seq, d), by_seq(y_s),
            kt_p.reshape(batch, N_HEADS, dh, seq).transpose(0, 3, 1, 2),
            vt_p.reshape(batch, N_HEADS, dh, seq).transpose(0, 3, 1, 2),
            kt_s.reshape(nq, N_HEADS, dh, n_seq).transpose(3, 0, 1, 2),
            vt_s.reshape(nq, N_HEADS, dh, n_seq).transpose(3, 0, 1, 2),
            sguv_p.reshape(1, batch, CHUNK, d), by_seq(sguv_s)[None])
```

```python
import functools

import numpy as np
import jax
import jax.numpy as jnp
from jax import lax
from jax.experimental import pallas as pl
from jax.experimental.pallas import tpu as pltpu

F32 = jnp.float32
BF16 = jnp.bfloat16
EPS = 1e-6
NEG_INF = float("-inf")
LOG2_E = 1.4426950408889634

V7X_VMEM_LIMIT_BYTES = 56 * 1024 * 1024
LANES = 128
SUBLANES = 8

N_HEADS = 16
MOBA_BLOCK = 256
MOBA_TOP_K = 3
CHUNK = 128
N_SGU_GROUPS = 8
PAGE_SIZE = 128
TOKEN_TILE = 512
ATTN_HEADS_PER_STEP = 8
ATTN_HEADS_PER_FUSED_STEP = 4


def _params(n_grid_dims):
    return pltpu.CompilerParams(
        dimension_semantics=("arbitrary",) * n_grid_dims,
        vmem_limit_bytes=V7X_VMEM_LIMIT_BYTES)


def _const_spec(shape):
    n = len(shape)
    return pl.BlockSpec(shape, lambda *_: (0,) * n, pipeline_mode=pl.Buffered(1))


def _layer_spec(w, layer):
    tail = (0,) * (w.ndim - 1)
    return pl.BlockSpec((None,) + w.shape[1:], lambda *_: (layer,) + tail, pipeline_mode=pl.Buffered(1))


def _mod_spec(mods, layer, n_seq, tiles_per_batch):
    w = mods.shape[-1]
    if tiles_per_batch:
        return pl.BlockSpec((None, SUBLANES, w), lambda *_: (layer, n_seq // SUBLANES, 0))
    return pl.BlockSpec((None, n_seq, w), lambda *_: (layer, 0, 0))


def _mod_rows(mod_ref, tiles_per_batch):
    if tiles_per_batch:
        return mod_ref[pl.ds(pl.program_id(0) // tiles_per_batch, 1), :]
    return mod_ref[...]


def _per_token(v, tm):
    if v.shape[0] in (1, tm):
        return v
    return jnp.concatenate([v] * (tm // v.shape[0]), axis=0)


def _dot(a, b):
    return jnp.dot(a, b, preferred_element_type=F32)


def _dot_nt(a, b):
    return lax.dot_general(a, b, (((1,), (1,)), ((), ())), preferred_element_type=F32)


def _split_bf16(x):
    hi = x.astype(BF16)
    lo = (x - hi.astype(F32)).astype(BF16)
    return hi, lo


def _rms(x, g):
    return x * lax.rsqrt(jnp.mean(x * x, axis=-1, keepdims=True) + EPS) * g


def _modulate(x, g, shift, scale):
    tm = x.shape[0]
    return _rms(x, g) * (1.0 + _per_token(scale, tm)) + _per_token(shift, tm)


def _head_rms(x, gain, pool_ref, expand_ref):
    sq_hi, sq_lo = _split_bf16(x * x)
    pool = pool_ref[...]
    ms = _dot(sq_hi, pool) + _dot(sq_lo, pool)
    r_hi, r_lo = _split_bf16(lax.rsqrt(ms + EPS))
    expand = expand_ref[...]
    return x * (_dot(r_hi, expand) + _dot(r_lo, expand)) * gain


def _head_rms_t(xt, gain_col):
    d, tm = xt.shape
    x3 = xt.reshape(N_HEADS, d // N_HEADS, tm)
    x3 = x3 * lax.rsqrt(jnp.mean(x3 * x3, axis=1, keepdims=True) + EPS)
    return x3.reshape(d, tm) * gain_col


def _ada_kernel(c_ref, w_ref, b_ref, o_ref):
    c = c_ref[...]
    s = (c * jax.nn.sigmoid(c)).astype(BF16)
    o_ref[...] = _dot(s, w_ref[...].astype(BF16)) + b_ref[pl.ds(pl.program_id(0), 1), :]


def _ada(c_all, w, b, tn):
    n_layers, d, n = w.shape
    m = c_all.shape[0]
    return pl.pallas_call(
        _ada_kernel,
        grid=(n_layers, n // tn),
        in_specs=[
            pl.BlockSpec((m, d), lambda l, j: (0, 0)),
            pl.BlockSpec((None, d, tn), lambda l, j: (l, 0, j)),
            pl.BlockSpec((n_layers, tn), lambda l, j: (0, j)),
        ],
        out_specs=pl.BlockSpec((None, m, tn), lambda l, j: (l, 0, j)),
        out_shape=jax.ShapeDtypeStruct((n_layers, m, n), F32),
        compiler_params=_params(2),
        name="ada",
    )(c_all, w, b)


def _sgu_front(x_ref, mod, g_ref, win_ref, vg_ref):
    d = x_ref.shape[1]
    h = _modulate(x_ref[...], g_ref[0:1, :], mod[:, 0:d], mod[:, d:2 * d]).astype(BF16)
    uv = jax.nn.gelu(_dot(h, win_ref[...]))
    return uv[:, :d], _rms(uv[:, d:], vg_ref[...])


def _sgu_prompt_kernel(tiles_per_batch, x_ref, mod_ref, g_ref, win_ref, vg_ref, wmix_ref, bmix_ref, uz_ref, v_ref):
    tm, d = x_ref.shape
    c_rows = wmix_ref.shape[1]
    gd = d // N_SGU_GROUPS
    u, v = _sgu_front(x_ref, _mod_rows(mod_ref, tiles_per_batch), g_ref, win_ref, vg_ref)
    v_ref[...] = v[tm - v_ref.shape[0]:, :]
    vb = v.astype(BF16)
    causal = (lax.broadcasted_iota(jnp.int32, (c_rows, c_rows), 1)
              <= lax.broadcasted_iota(jnp.int32, (c_rows, c_rows), 0))
    for g in range(N_SGU_GROUPS):
        wg = jnp.where(causal, wmix_ref[g], 0.0).astype(BF16)
        cols = slice(g * gd, (g + 1) * gd)
        for ch in range(tm // c_rows):
            rows = slice(ch * c_rows, (ch + 1) * c_rows)
            mixed = _dot(wg, vb[rows, cols]) + bmix_ref[:, cols]
            uz_ref[rows, cols] = (u[rows, cols] * mixed).astype(BF16)


def _sgu_sample_kernel(nq, x_ref, mod_ref, g_ref, win_ref, vg_ref, wcoef_ref, bcoef_ref, uz_ref, v_ref):
    tm = x_ref.shape[0]
    ns = tm // nq
    u, v = _sgu_front(x_ref, mod_ref[...], g_ref, win_ref, vg_ref)
    v_ref[...] = v
    for t in range(nq):
        mixed = bcoef_ref[t:t + 1, :]
        for s in range(t + 1):
            mixed = mixed + wcoef_ref[t * nq + s:t * nq + s + 1, :] * v[s * ns:(s + 1) * ns, :]
        uz_ref[t * ns:(t + 1) * ns, :] = (u[t * ns:(t + 1) * ns, :] * mixed).astype(BF16)


def _ff_chunks(d_ff):
    step = 768
    return tuple((s, min(step, d_ff - s)) for s in range(0, d_ff, step))


def _ffn_kernel(layer, tiles_per_batch, x_ref, z_ref, mod_ref, wp_ref, g_ref, win_ref, wout_ref, o_ref):
    tm, d = x_ref.shape
    d_ff = wout_ref.shape[0]
    mod = _mod_rows(mod_ref, tiles_per_batch)
    gate1 = _per_token(mod[:, 2 * d:3 * d], tm)
    gate2 = _per_token(mod[:, 5 * d:6 * d], tm)
    x1 = x_ref[...] + gate1 * _dot(z_ref[...].astype(BF16), wp_ref[...])
    h = _modulate(x1, g_ref[layer:layer + 1, :], mod[:, 3 * d:4 * d], mod[:, 4 * d:5 * d]).astype(BF16)
    acc = None
    for s, w in _ff_chunks(d_ff):
        a = _dot(h, win_ref[:, s:s + w])
        b = _dot(h, win_ref[:, d_ff + s:d_ff + s + w])
        act = (a * jax.nn.sigmoid(a) * b).astype(BF16)
        part = _dot(act, wout_ref[s:s + w, :])
        acc = part if acc is None else acc + part
    o_ref[...] = x1 + gate2 * acc


def _ffn(x, z, layer, mods, n_seq, tiles_per_batch, w_proj, norm_g, w_in, w_out):
    t, d = x.shape
    tm = TOKEN_TILE
    return pl.pallas_call(
        functools.partial(_ffn_kernel, layer, tiles_per_batch),
        grid=(t // tm,),
        in_specs=[
            pl.BlockSpec((tm, d), lambda i: (i, 0)),
            pl.BlockSpec((tm, d), lambda i: (i, 0)),
            _mod_spec(mods, layer, n_seq, tiles_per_batch),
            _const_spec(w_proj.shape),
            _const_spec(norm_g.shape),
            _layer_spec(w_in, layer),
            _layer_spec(w_out, layer),
        ],
        out_specs=pl.BlockSpec((tm, d), lambda i: (i, 0)),
        out_shape=jax.ShapeDtypeStruct((t, d), F32),
        compiler_params=_params(1),
        name="ffn",
    )(x, z, mods, w_proj, norm_g, w_in, w_out)


def _layer0_prompt_kernel(tiles_per_batch, x_ref, mod_ref, g1_ref, swin_ref, vg_ref, wmix_ref, bmix_ref, wp_ref,
                          g2_ref, win_ref, wout_ref, o_ref, v_ref, uz_ref):
    _sgu_prompt_kernel(tiles_per_batch, x_ref, mod_ref, g1_ref, swin_ref, vg_ref, wmix_ref, bmix_ref, uz_ref, v_ref)
    _ffn_kernel(0, tiles_per_batch, x_ref, uz_ref, mod_ref, wp_ref, g2_ref, win_ref, wout_ref, o_ref)


def _layer0_prompt(x, batch, mods, n_seq, norm1_g, sgu_w_in, v_g, wmix, bmix, w_proj, norm2_g, w_in, w_out):
    t, d = x.shape
    tm = TOKEN_TILE
    tiles_per_batch = t // batch // tm
    return pl.pallas_call(
        functools.partial(_layer0_prompt_kernel, tiles_per_batch),
        grid=(t // tm,),
        in_specs=[
            pl.BlockSpec((tm, d), lambda i: (i, 0)),
            _mod_spec(mods, 0, n_seq, tiles_per_batch),
            _const_spec(norm1_g.shape),
            _const_spec(sgu_w_in.shape),
            _const_spec(v_g.shape),
            _const_spec(wmix.shape),
            _const_spec(bmix.shape),
            _const_spec(w_proj.shape),
            _const_spec(norm2_g.shape),
            _layer_spec(w_in, 0),
            _layer_spec(w_out, 0),
        ],
        out_specs=[pl.BlockSpec((tm, d), lambda i: (i, 0)),
                   pl.BlockSpec((None, CHUNK, d), lambda i: (i // tiles_per_batch, 0, 0))],
        out_shape=[jax.ShapeDtypeStruct((t, d), F32), jax.ShapeDtypeStruct((batch, CHUNK, d), F32)],
        scratch_shapes=[pltpu.VMEM((tm, d), BF16)],
        compiler_params=_params(1),
        name="layer0_prompt",
    )(x, mods, norm1_g, sgu_w_in, v_g, wmix, bmix, w_proj, norm2_g, w_in, w_out)


def _layer0_sample_kernel(nq, x_ref, mod_ref, g1_ref, swin_ref, vg_ref, wcoef_ref, bcoef_ref, wp_ref, g2_ref,
                          win_ref, wout_ref, o_ref, v_ref, uz_ref):
    _sgu_sample_kernel(nq, x_ref, mod_ref, g1_ref, swin_ref, vg_ref, wcoef_ref, bcoef_ref, uz_ref, v_ref)
    _ffn_kernel(0, 0, x_ref, uz_ref, mod_ref, wp_ref, g2_ref, win_ref, wout_ref, o_ref)


def _layer0_sample(x, nq, mods, n_seq, norm1_g, sgu_w_in, v_g, wcoef, bcoef, w_proj, norm2_g, w_in, w_out):
    t, d = x.shape
    full = pl.BlockSpec((t, d), lambda i: (0, 0))
    return pl.pallas_call(
        functools.partial(_layer0_sample_kernel, nq),
        grid=(1,),
        in_specs=[
            full,
            _mod_spec(mods, 0, n_seq, 0),
            _const_spec(norm1_g.shape),
            _const_spec(sgu_w_in.shape),
            _const_spec(v_g.shape),
            _const_spec(wcoef.shape),
            _const_spec(bcoef.shape),
            _const_spec(w_proj.shape),
            _const_spec(norm2_g.shape),
            _layer_spec(w_in, 0),
            _layer_spec(w_out, 0),
        ],
        out_specs=[full, full],
        out_shape=[jax.ShapeDtypeStruct((t, d), F32), jax.ShapeDtypeStruct((t, d), F32)],
        scratch_shapes=[pltpu.VMEM((t, d), BF16)],
        compiler_params=_params(1),
        name="layer0_sample",
    )(x, mods, norm1_g, sgu_w_in, v_g, wcoef, bcoef, w_proj, norm2_g, w_in, w_out)


def _kvq_prompt_kernel(tiles_per_batch, x_ref, kvmod_ref, mod_ref, kvg_ref, n1g_ref, wkt_ref, wvt_ref, wqt_ref,
                       kg_ref, qg_ref, kt_ref, vt_ref, kb_ref, vtb_ref, qt_ref, kmean_ref):
    tm, d = x_ref.shape
    x = x_ref[...]
    kvmod = _mod_rows(kvmod_ref, tiles_per_batch)
    mod = _mod_rows(mod_ref, tiles_per_batch)
    hkv = _modulate(x, kvg_ref[...], kvmod[:, 0:d], kvmod[:, d:2 * d]).astype(BF16)
    kt = _head_rms_t(_dot_nt(wkt_ref[...], hkv), kg_ref[...])
    vt = _dot_nt(wvt_ref[...], hkv)
    kt_ref[...] = kt
    vt_ref[...] = vt
    vtb_ref[...] = vt.astype(BF16)
    hq = _modulate(x, n1g_ref[1:2, :], mod[:, 0:d], mod[:, d:2 * d]).astype(BF16)
    qt_ref[...] = _head_rms_t(_dot_nt(wqt_ref[...], hq), qg_ref[...]).astype(BF16)
    for i in range(tm // MOBA_BLOCK):
        cols = slice(i * MOBA_BLOCK, (i + 1) * MOBA_BLOCK)
        k_nat = kt[:, cols].T
        kmean_ref[i] = jnp.mean(k_nat, axis=0, keepdims=True)
        kb_ref[cols, :] = k_nat.astype(BF16)


def _kvq_prompt(x, batch, kvmods, mods, n_seq, kv_norm_g, norm1_g, w_kt, w_vt, w_qt, k_gain_col, q_gain_col):
    t, d = x.shape
    tm = TOKEN_TILE
    seq = t // batch
    tiles_per_batch = seq // tm
    nblk = tm // MOBA_BLOCK
    row_spec = pl.BlockSpec((tm, d), lambda i: (i, 0))
    col_spec = pl.BlockSpec((None, d, tm), lambda i: (i // tiles_per_batch, 0, i % tiles_per_batch))
    return pl.pallas_call(
        functools.partial(_kvq_prompt_kernel, tiles_per_batch),
        grid=(t // tm,),
        in_specs=[
            row_spec,
            _mod_spec(kvmods, 0, n_seq, tiles_per_batch),
            _mod_spec(mods, 1, n_seq, tiles_per_batch),
            _const_spec(kv_norm_g.shape), _const_spec(norm1_g.shape),
            _const_spec(w_kt.shape), _const_spec(w_vt.shape), _const_spec(w_qt.shape),
            _const_spec((d, 1)), _const_spec((d, 1)),
        ],
        out_specs=[
            col_spec, col_spec, row_spec, col_spec, col_spec,
            pl.BlockSpec((nblk, 1, d), lambda i: (i, 0, 0)),
        ],
        out_shape=[
            jax.ShapeDtypeStruct((batch, d, seq), F32),
            jax.ShapeDtypeStruct((batch, d, seq), F32),
            jax.ShapeDtypeStruct((t, d), BF16),
            jax.ShapeDtypeStruct((batch, d, seq), BF16),
            jax.ShapeDtypeStruct((batch, d, seq), BF16),
            jax.ShapeDtypeStruct((t // MOBA_BLOCK, 1, d), F32),
        ],
        compiler_params=_params(1),
        name="kvq_prompt",
    )(x, kvmods, mods, kv_norm_g, norm1_g, w_kt, w_vt, w_qt, k_gain_col, q_gain_col)


def _kvq_sample_kernel(nq, x_ref, kvmod_ref, mod_ref, kvg_ref, n1g_ref, wk_ref, wv_ref, wq_ref, kg_ref, qg_ref,
                       pool_ref, expand_ref, k_ref, v_ref, q_ref, kt_ref, vt_ref):
    tm, d = x_ref.shape
    ns = tm // nq
    x = x_ref[...]
    kvmod = kvmod_ref[...]
    mod = mod_ref[...]
    hkv = _modulate(x, kvg_ref[...], kvmod[:, 0:d], kvmod[:, d:2 * d]).astype(BF16)
    k = _head_rms(_dot(hkv, wk_ref[...]), kg_ref[...], pool_ref, expand_ref)
    v = _dot(hkv, wv_ref[...])
    k_ref[...] = k
    v_ref[...] = v
    for t in range(nq):
        kt_ref[t] = k[t * ns:(t + 1) * ns, :].T
        vt_ref[t] = v[t * ns:(t + 1) * ns, :].T
    hq = _modulate(x, n1g_ref[1:2, :], mod[:, 0:d], mod[:, d:2 * d]).astype(BF16)
    q_ref[...] = _head_rms(_dot(hq, wq_ref[...]), qg_ref[...], pool_ref, expand_ref)


def _kvq_sample(x, nq, kvmods, mods, n_seq, kv_norm_g, norm1_g, w_k, w_v, w_q, k_gain, q_gain, pool, expand):
    t, d = x.shape
    full = pl.BlockSpec((t, d), lambda i: (0, 0))
    tspec = pl.BlockSpec((nq, d, t // nq), lambda i: (0, 0, 0))
    return pl.pallas_call(
        functools.partial(_kvq_sample_kernel, nq),
        grid=(1,),
        in_specs=[
            full,
            _mod_spec(kvmods, 0, n_seq, 0),
            _mod_spec(mods, 1, n_seq, 0),
            _const_spec(kv_norm_g.shape), _const_spec(norm1_g.shape),
            _const_spec(w_k.shape), _const_spec(w_v.shape), _const_spec(w_q.shape),
            _const_spec((1, d)), _const_spec((1, d)),
            _const_spec(pool.shape), _const_spec(expand.shape),
        ],
        out_specs=[full, full, full, tspec, tspec],
        out_shape=[jax.ShapeDtypeStruct((t, d), F32)] * 3 + [jax.ShapeDtypeStruct((nq, d, t // nq), F32)] * 2,
        compiler_params=_params(1),
        name="kvq_sample",
    )(x, kvmods, mods, kv_norm_g, norm1_g, w_k, w_v, w_q, k_gain, q_gain, pool, expand)


def _top_k_additive_mask(gate, own, axis):
    nb = gate.shape[axis]
    idx = lax.broadcasted_iota(jnp.int32, gate.shape, axis)
    eligible = idx < own
    g = jnp.where(eligible, gate, NEG_INF)
    mask = jnp.full(gate.shape, NEG_INF, F32)
    for _ in range(min(MOBA_TOP_K, nb)):
        best = jnp.max(g, axis=axis, keepdims=True)
        first = jnp.min(jnp.where(g == best, idx, nb), axis=axis, keepdims=True)
        pick = idx == first
        mask = jnp.where(pick, 0.0, mask)
        g = jnp.where(pick, NEG_INF, g)
    return jnp.where(eligible, mask, NEG_INF)


def _prompt_attention(j, slopes_ref, qt_ref, kb_ref, vt_ref, kmean_ref, o_ref, hooks):
    group = pl.program_id(1)
    blk = MOBA_BLOCK
    pair_w = LANES
    n_heads = 2 * (qt_ref.shape[0] // pair_w)
    dh = pair_w // 2
    zeros = jnp.zeros((dh, blk), BF16)
    causal = (lax.broadcasted_iota(jnp.int32, (blk, blk), 0)
              <= lax.broadcasted_iota(jnp.int32, (blk, blk), 1))
    n_feat = 3
    k_feat = jnp.where(lax.broadcasted_iota(jnp.int32, (blk, LANES), 1) < n_feat,
                       lax.broadcasted_iota(jnp.int32, (blk, LANES), 0), 0).astype(F32).astype(BF16)
    feat_row = lax.broadcasted_iota(jnp.int32, (LANES, blk), 0)
    ones_rows = jnp.ones((16, blk), BF16)

    def prepare(h):
        lanes = slice((h // 2) * pair_w, (h // 2 + 1) * pair_w)
        slope = slopes_ref[group * n_heads + h]
        qt = qt_ref[lanes, :]
        qh = jnp.concatenate([qt[:dh], zeros] if h % 2 == 0 else [zeros, qt[dh:]], axis=0)
        q_feat = jnp.zeros((LANES, blk), F32)
        rest_slope = jnp.full((1, blk), slope, F32)
        for f in range(n_feat):
            piece = rest_slope.astype(BF16).astype(F32)
            q_feat = jnp.where(feat_row == f, piece, q_feat)
            rest_slope = rest_slope - piece
        q_aug = jnp.concatenate([qh, q_feat.astype(BF16)], axis=0)
        rows = []
        if j > 0:
            km_hi, km_lo = _split_bf16(kmean_ref[:, lanes])
            sel = _top_k_additive_mask(_dot(km_hi, qh) + _dot(km_lo, qh), j, 0)
            rows = [sel[i:i + 1, :] + slope * float((i - j) * blk) for i in range(j)]
        rows.append(jnp.zeros((1, blk), F32))
        return q_aug, rows

    def scores(h, q_aug, i):
        lanes = slice((h // 2) * pair_w, (h // 2 + 1) * pair_w)
        t = _dot(jnp.concatenate([kb_ref[i * blk:(i + 1) * blk, lanes], k_feat], axis=1), q_aug)
        return jnp.where(causal, t, NEG_INF) if i == j else t

    def values(h, t, shift, i):
        p = jnp.exp2((t - shift).astype(BF16))
        vt_aug = jnp.concatenate([vt_ref[h * dh:(h + 1) * dh, i * blk:(i + 1) * blk], ones_rows], axis=0)
        return _dot(vt_aug, p)

    outs = []
    prev = None
    for h in range(n_heads + 1):
        if h in hooks:
            hooks[h]()
        if h < n_heads:
            q_aug, rows = prepare(h)
            ts, m = [], None
        acc = None
        for i in range(j + 1):
            if h < n_heads:
                t = scores(h, q_aug, i)
                ts.append(t)
                mi = jnp.max(t, axis=0, keepdims=True) + rows[i]
                m = mi if m is None else jnp.maximum(m, mi)
            if prev is not None:
                p_ts, p_m, p_rows = prev
                part = values(h - 1, p_ts[i], p_m - p_rows[i], i)
                acc = part if acc is None else acc + part
        if prev is not None:
            outs.append(acc[:dh] / acc[dh:dh + 1])
        prev = (ts, m, rows) if h < n_heads else None
    o_ref[...] = jnp.concatenate(outs, axis=0).T.astype(o_ref.dtype)


def _sample_attention(n_pages, q_ref, kn_ref, vn_ref, slope_ref, kbuf, vbuf, o_ref):
    nq, _, d = q_ref.shape
    dh = d // N_HEADS
    rows = nq * N_HEADS
    pages_per_block = MOBA_BLOCK // PAGE_SIZE
    n_past = n_pages // pages_per_block
    past_len = n_pages * PAGE_SIZE

    lane_head = lax.broadcasted_iota(jnp.int32, (N_HEADS, d), 1) // dh
    head_mask = lane_head == lax.broadcasted_iota(jnp.int32, (N_HEADS, d), 0)
    qrows = jnp.concatenate([jnp.where(head_mask, q_ref[i], 0.0) for i in range(nq)], axis=0)
    qrows_b = qrows.astype(BF16)
    row_head_mask = jnp.concatenate([head_mask] * nq, axis=0)
    slope = slope_ref[...]
    row_q = lax.broadcasted_iota(jnp.int32, (rows, 1), 0) // N_HEADS

    scores = [_dot(qrows_b, kbuf[p].astype(BF16)) for p in range(n_pages)]
    gate = jnp.concatenate(
        [sum(jnp.sum(s, axis=1, keepdims=True) for s in scores[n * pages_per_block:(n + 1) * pages_per_block])
         for n in range(n_past)], axis=1) * (1.0 / MOBA_BLOCK)
    sel = _top_k_additive_mask(gate, n_past, 1)

    dist0 = (past_len + row_q).astype(F32)
    offs = lax.broadcasted_iota(jnp.int32, (1, PAGE_SIZE), 1)
    masked = []
    for p in range(n_pages):
        n = p // pages_per_block
        kpos = (p * PAGE_SIZE + offs).astype(F32)
        masked.append(scores[p] - slope * (dist0 - kpos) + sel[:, n:n + 1])
    own = []
    for t in range(nq):
        s = jnp.sum(qrows * kn_ref[t], axis=1, keepdims=True)
        s = s - slope * (row_q - t).astype(F32)
        own.append(jnp.where(row_q >= t, s, NEG_INF))
    m = own[0]
    for s in own[1:]:
        m = jnp.maximum(m, s)
    for s in masked:
        m = jnp.maximum(m, jnp.max(s, axis=1, keepdims=True))
    l = jnp.zeros((rows, 1), F32)
    acc = jnp.zeros((rows, d), F32)
    for t in range(nq):
        p_own = jnp.exp(own[t] - m)
        l = l + p_own
        acc = acc + p_own * vn_ref[t]
    for p in range(n_pages):
        pr = jnp.exp(masked[p] - m)
        l = l + jnp.sum(pr, axis=1, keepdims=True)
        acc = acc + _dot_nt(pr.astype(BF16), vbuf[p].astype(BF16))
    o_rows = jnp.where(row_head_mask, acc / l, 0.0)
    o = jnp.sum(o_rows.reshape(nq, N_HEADS, d), axis=1)
    for t in range(nq):
        o_ref[t] = o[t:t + 1, :]


def _attn_kernel(j, fused, seq_base, n_pages, pt_ref, slopes_ref, qt_ref, kb_ref, vt_ref, kmean_ref,
                 o_in_ref, *rest):
    if not fused:
        (o_ref,) = rest
        _prompt_attention(j, slopes_ref, qt_ref, kb_ref, vt_ref, kmean_ref, o_ref, {})
        return
    qs_ref, kn_ref, vn_ref, srow_ref, ck_ref, cv_ref, os_in_ref, o_ref, os_ref, kbuf, vbuf, sem = rest
    n_steps = pl.num_programs(0) * pl.num_programs(1)
    step = pl.program_id(0) * pl.num_programs(1) + pl.program_id(1)
    slot = step % 2

    def page_copies(at_step, at_slot):
        seq = seq_base + at_step
        copies = []
        for i in range(n_pages):
            page = pt_ref[seq * n_pages + i]
            copies.append(pltpu.make_async_copy(ck_ref.at[page], kbuf.at[at_slot, i], sem.at[at_slot, 0, i]))
            copies.append(pltpu.make_async_copy(cv_ref.at[page], vbuf.at[at_slot, i], sem.at[at_slot, 1, i]))
        return copies

    @pl.when(step == 0)
    def _():
        for c in page_copies(step, slot):
            c.start()

    @pl.when(step < n_steps)
    def _():
        for c in page_copies(step, slot):
            c.wait()

    @pl.when(step + 1 < n_steps)
    def _():
        for c in page_copies(step + 1, 1 - slot):
            c.start()

    _prompt_attention(j, slopes_ref, qt_ref, kb_ref, vt_ref, kmean_ref, o_ref, {})
    _sample_attention(n_pages, qs_ref, kn_ref, vn_ref, srow_ref, kbuf.at[slot], vbuf.at[slot], os_ref)


def _attention(qt, kb, vt, kmean, slopes, page_table, q_s, k_s, v_s, slope_rows, cache_k, cache_v):
    batch, d, seq = qt.shape
    nq, n_seq, _ = q_s.shape
    blk = MOBA_BLOCK
    n_blocks = seq // blk
    dh = d // N_HEADS
    n_pages = page_table.shape[1]
    pool_pages = cache_k.shape[0]
    ck = cache_k.transpose(0, 2, 3, 1).reshape(pool_pages, d, PAGE_SIZE)
    cv = cache_v.transpose(0, 2, 3, 1).reshape(pool_pages, d, PAGE_SIZE)
    per_seq = lambda a: a.reshape(nq, n_seq, 1, d)
    fused_steps = batch * (N_HEADS // ATTN_HEADS_PER_FUSED_STEP)
    assert n_seq % fused_steps == 0 and n_seq // fused_steps <= n_blocks
    first_fused = n_blocks - n_seq // fused_steps
    pt = page_table.reshape(-1)
    out = jnp.zeros((batch, seq, d), BF16)
    out_s = jnp.zeros((nq, n_seq, 1, d), F32)
    for j in range(n_blocks):
        keys = (j + 1) * blk
        fused = j >= first_fused
        width = (ATTN_HEADS_PER_FUSED_STEP if fused else ATTN_HEADS_PER_STEP) * dh
        groups = d // width
        seq_base = (j - first_fused) * fused_steps
        in_specs = [
            pl.BlockSpec(memory_space=pltpu.SMEM),
            pl.BlockSpec((None, width, blk), lambda b, g, pt, j=j: (b, g, j)),
            pl.BlockSpec((None, keys, width), lambda b, g, pt: (b, 0, g)),
            pl.BlockSpec((None, width, keys), lambda b, g, pt: (b, g, 0)),
            pl.BlockSpec((None, n_blocks, width), lambda b, g, pt: (b, 0, g)),
            pl.BlockSpec(memory_space=pl.ANY),
        ]
        args = [pt, slopes, qt, kb, vt, kmean, out]
        out_specs = [pl.BlockSpec((None, blk, width), lambda b, g, pt, j=j: (b, j, g))]
        out_shape = [jax.ShapeDtypeStruct((batch, seq, d), BF16)]
        scratch = []
        aliases = {len(args) - 1: 0}
        if fused:
            seq_spec = pl.BlockSpec((nq, None, 1, d),
                                    lambda b, g, pt, s0=seq_base, groups=groups: (0, s0 + b * groups + g, 0, 0))
            in_specs += [seq_spec, seq_spec, seq_spec,
                         pl.BlockSpec(slope_rows.shape, lambda b, g, pt: (0, 0)),
                         pl.BlockSpec(memory_space=pl.ANY), pl.BlockSpec(memory_space=pl.ANY),
                         pl.BlockSpec(memory_space=pl.ANY)]
            args += [per_seq(q_s), per_seq(k_s), per_seq(v_s), slope_rows, ck, cv, out_s]
            aliases[len(args) - 1] = 1
            out_specs.append(seq_spec)
            out_shape.append(jax.ShapeDtypeStruct((nq, n_seq, 1, d), F32))
            scratch = [pltpu.VMEM((2, n_pages, d, PAGE_SIZE), F32), pltpu.VMEM((2, n_pages, d, PAGE_SIZE), F32),
                       pltpu.SemaphoreType.DMA((2, 2, n_pages))]
        res = pl.pallas_call(
            functools.partial(_attn_kernel, j, fused, seq_base, n_pages),
            grid_spec=pltpu.PrefetchScalarGridSpec(
                num_scalar_prefetch=1, grid=(batch, groups), in_specs=in_specs, out_specs=out_specs,
                scratch_shapes=scratch),
            out_shape=out_shape,
            input_output_aliases=aliases,
            compiler_params=_params(2),
            name=f"attn_{j}",
        )(*args)
        out = res[0]
        if fused:
            out_s = res[1]
    return out, out_s


def kernel(x_prompt, x_sample, c_prompt, c_sample, cache_k, cache_v, page_table, ada_w, ada_b, norm1_g, norm2_g,
           ffn_w_in, ffn_w_out, sgu_w_in, sgu_v_g, sgu_w_s, sgu_b_s, sgu_w_out, kv_ada_w, kv_ada_b, kv_norm_g,
           w_kv, k_norm_g, attn_w_q, q_norm_g, attn_w_o):
    batch, seq, d = x_prompt.shape
    n_seq, nq, _ = x_sample.shape
    dh = d // N_HEADS
    gd = d // N_SGU_GROUPS
    n_tok_p = batch * seq
    n_tok_s = n_seq * nq
    assert seq % TOKEN_TILE == 0 and n_tok_s == TOKEN_TILE and n_seq % SUBLANES == 0 and batch <= SUBLANES

    c_all = jnp.concatenate([c_sample, c_prompt, jnp.zeros((SUBLANES - batch, d), F32)], axis=0)
    mods = _ada(c_all, ada_w, ada_b, 1536)
    kvmods = _ada(c_all, kv_ada_w[None], kv_ada_b[None], 1024)

    bf = lambda w: w.astype(BF16)
    sgu_w_in_b, sgu_w_out_b = bf(sgu_w_in[0]), bf(sgu_w_out[0])
    ffn_in_b, ffn_out_b = bf(ffn_w_in), bf(ffn_w_out)
    w_k_b, w_v_b = bf(w_kv[:, :d]), bf(w_kv[:, d:])
    w_q_b, w_qt_b, w_o_b = bf(attn_w_q[0]), bf(attn_w_q[0].T), bf(attn_w_o[0])
    k_gain = jnp.tile(k_norm_g, N_HEADS).reshape(1, d)
    q_gain = (jnp.tile(q_norm_g[0], N_HEADS) * (dh ** -0.5)).reshape(1, d)
    kv_norm_row = kv_norm_g.reshape(1, d)
    head_of_lane = np.arange(d) // dh
    pool = jnp.asarray((head_of_lane[:, None] == np.arange(LANES)[None, :]) / dh, BF16)
    expand = jnp.asarray(np.arange(LANES)[:, None] == head_of_lane[None, :], BF16)
    slopes = jnp.exp2(-8.0 * jnp.arange(1, N_HEADS + 1, dtype=F32) / N_HEADS)

    wmix_p = sgu_w_s[0]
    bmix_p = jnp.repeat(sgu_b_s[0].T, gd, axis=1)
    wcoef_s = jnp.repeat(sgu_w_s[0][:, :nq, :nq].transpose(1, 2, 0).reshape(nq * nq, N_SGU_GROUPS), gd, axis=1)
    bcoef_s = jnp.repeat(sgu_b_s[0][:, :nq].T, gd, axis=1)

    xp = x_prompt.reshape(n_tok_p, d)
    xs = x_sample.transpose(1, 0, 2).reshape(n_tok_s, d)
    tiles_pb = seq // TOKEN_TILE

    x2_p, sguv_p = _layer0_prompt(xp, batch, mods, n_seq, norm1_g, sgu_w_in_b, sgu_v_g, wmix_p, bmix_p,
                                  sgu_w_out_b, norm2_g, ffn_in_b, ffn_out_b)
    kt_p, vt_p, kb_p, vtb_p, qt_p, kmean_p = _kvq_prompt(
        x2_p, batch, kvmods, mods, n_seq, kv_norm_row, norm1_g, bf(w_kv[:, :d].T), bf(w_kv[:, d:].T), w_qt_b,
        k_gain.reshape(d, 1), (q_gain * LOG2_E).reshape(d, 1))
    x2_s, sguv_s = _layer0_sample(xs, nq, mods, n_seq, norm1_g, sgu_w_in_b, sgu_v_g, wcoef_s, bcoef_s,
                                  sgu_w_out_b, norm2_g, ffn_in_b, ffn_out_b)
    k_s, v_s, q_s, kt_s, vt_s = _kvq_sample(x2_s, nq, kvmods, mods, n_seq, kv_norm_row, norm1_g, w_k_b, w_v_b,
                                            w_q_b, k_gain, q_gain, pool, expand)

    slope_rows = jnp.tile(slopes, nq).reshape(nq * N_HEADS, 1)
    o_p, o_s = _attention(qt_p, kb_p.reshape(batch, seq, d), vtb_p, kmean_p.reshape(batch, seq // MOBA_BLOCK, d),
                          slopes * LOG2_E, page_table, q_s.reshape(nq, n_seq, d), k_s.reshape(nq, n_seq, d),
                          v_s.reshape(nq, n_seq, d), slope_rows, cache_k, cache_v)
    y_p = _ffn(x2_p, o_p.reshape(n_tok_p, d), 1, mods, n_seq, tiles_pb, w_o_b, norm2_g, ffn_in_b, ffn_out_b)
    y_s = _ffn(x2_s, o_s.reshape(n_tok_s, d), 1, mods, n_seq, 0, w_o_b, norm2_g, ffn_in_b, ffn_out_b)

    by_seq = lambda a: a.reshape(nq, n_seq, d).transpose(1, 0, 2)
    return (y_p.reshape(batch, seq, d), by_seq(y_s),
            kt_p.reshape(batch, N_HEADS, dh, seq).transpose(0, 3, 1, 2),
            vt_p.reshape(batch, N_HEADS, dh, seq).transpose(0, 3, 1, 2),
            kt_s.reshape(nq, N_HEADS, dh, n_seq).transpose(3, 0, 1, 2),
            vt_s.reshape(nq, N_HEADS, dh, n_seq).transpose(3, 0, 1, 2),
            sguv_p.reshape(1, batch, CHUNK, d), by_seq(sguv_s)[None])
```
